```python
import jax
import jax.numpy as jnp
from jax import lax
import numpy as np

D_MODEL = 4096
BATCH = 2
SEQ = 8192
DEPTH = 2

GRID_W = 64
CTX_LEN = 256

N_HEADS = 16
Q_LORA = 1024
KV_LORA = 512
QK_NOPE = 128
QK_ROPE = 64
V_DIM = 128
ATTN_W = N_HEADS * V_DIM
ROPE_BASE = 10000.0
Q_BLOCK = 128

POOL_WINDOWS = (2, 4, 8, 16)
POOL_W = 1024
POOL_GW = POOL_W // len(POOL_WINDOWS)

GM_CHUNK = 128
GM_GROUPS = 4
GM_W = 1024
GM_GW = GM_W // GM_GROUPS

N_BRANCH = 3
MIX_W = ATTN_W + POOL_W + GM_W

KV_START = Q_LORA
KV_END = KV_START + KV_LORA + QK_ROPE
POOL_START = KV_END
POOL_END = POOL_START + POOL_W
GM_START = POOL_END
GM_END = GM_START + 2 * GM_W
GATE_START = GM_END
IN_COLS = GATE_START + N_BRANCH * D_MODEL

N_EXPERTS = 16
CAPACITY_FACTOR = 2
D_EXPERT = 1024

EPS = 1e-6

kernel_name = 'hybrid_mla_pool_gmlp_ec_dit_block'


def _rms(x):
    xf = x.astype(jnp.float32)
    return (xf * lax.rsqrt(jnp.mean(xf * xf, axis=-1, keepdims=True) + EPS)).astype(x.dtype)


def _modulate(x, shift, scale):
    return _rms(x) * (1 + scale) + shift


def _axial_rope_tables(n):
    rows = n // GRID_W
    row = jnp.broadcast_to(jnp.arange(rows, dtype=jnp.float32)[:, None], (rows, GRID_W)).reshape(-1)
    col = jnp.broadcast_to(jnp.arange(GRID_W, dtype=jnp.float32)[None, :], (rows, GRID_W)).reshape(-1)
    n_freq = QK_ROPE // 4
    inv = ROPE_BASE ** (-jnp.arange(n_freq, dtype=jnp.float32) / n_freq)
    ang_r = (row[:, None] * inv)[None, :, None, :]
    ang_c = (col[:, None] * inv)[None, :, None, :]
    return (jnp.cos(ang_r), jnp.sin(ang_r), jnp.cos(ang_c), jnp.sin(ang_c))


def _rotate(x, cos, sin):
    x1, x2 = jnp.split(x, 2, axis=-1)
    return jnp.concatenate([x1 * cos - x2 * sin, x2 * cos + x1 * sin], axis=-1)


def _axial_rope(x, tables):
    cr, sr, cc, sc = tables
    half = QK_ROPE // 2
    out = jnp.concatenate([_rotate(x[..., :half], cr, sr), _rotate(x[..., half:], cc, sc)], axis=-1)
    return out.astype(x.dtype)


def _mla_queries(proj, q_norm_g, w_uq):
    b, n, _ = proj.shape
    cq = _rms(proj[..., :Q_LORA]) * q_norm_g
    q = (cq @ w_uq).reshape(b, n, N_HEADS, QK_NOPE + QK_ROPE)
    return q[..., :QK_NOPE], q[..., QK_NOPE:]


def _mla_keys(kvr, kv_norm_g, w_ukv):
    b, n, _ = kvr.shape
    ckv = _rms(kvr[..., :KV_LORA]) * kv_norm_g
    kv = (ckv @ w_ukv).reshape(b, n, N_HEADS, QK_NOPE + V_DIM)
    return kv[..., :QK_NOPE], kvr[..., KV_LORA:], kv[..., QK_NOPE:]


def _block_attention(q_nope, q_rope, k_nope, k_rope, v):
    b, n, h, _ = q_nope.shape
    nb = n // Q_BLOCK
    scale = (QK_NOPE + QK_ROPE) ** -0.5
    qn = q_nope.reshape(b, nb, Q_BLOCK, h, QK_NOPE).transpose(1, 0, 2, 3, 4)
    qr = q_rope.reshape(b, nb, Q_BLOCK, h, QK_ROPE).transpose(1, 0, 2, 3, 4)

    def one_block(qs):
        qn_b, qr_b = qs
        s = jnp.einsum('bqhd,bkhd->bhqk', qn_b, k_nope) + jnp.einsum('bqhr,bkr->bhqk', qr_b, k_rope)
        p = jax.nn.softmax(s.astype(jnp.float32) * scale, axis=-1).astype(v.dtype)
        return jnp.einsum('bhqk,bkhd->bqhd', p, v)

    out = lax.map(one_block, (qn, qr))
    return out.transpose(1, 0, 2, 3, 4).reshape(b, n, h * V_DIM)


def _multiscale_pool(p, w_pool, pool_scale):
    b, n, _ = p.shape
    pf = p.astype(jnp.float32)
    cs = jnp.concatenate([jnp.zeros((b, 1, POOL_W), jnp.float32), jnp.cumsum(pf, axis=1)], axis=1)
    t = jnp.arange(n)
    outs = []
    for g, win in enumerate(POOL_WINDOWS):
        left = win // 2
        right = win - 1 - left
        lo = jnp.clip(t - left, 0, n)
        hi = jnp.clip(t + right + 1, 0, n)
        sl = slice(g * POOL_GW, (g + 1) * POOL_GW)
        mean = (cs[:, hi, sl] - cs[:, lo, sl]) / (hi - lo).astype(jnp.float32)[None, :, None]
        outs.append((mean - pf[..., sl]).astype(p.dtype) @ w_pool[g])
    return jnp.concatenate(outs, axis=-1) * pool_scale


def _spatial_gating(z, w_s, b_s):
    b, n, _ = z.shape
    z = jax.nn.gelu(z, approximate=False)
    u, v = z[..., :GM_W], z[..., GM_W:]
    vf = v.astype(jnp.float32)
    mu = jnp.mean(vf, axis=-1, keepdims=True)
    var = jnp.mean(jnp.square(vf - mu), axis=-1, keepdims=True)
    vn = ((vf - mu) * lax.rsqrt(var + EPS)).astype(v.dtype)
    vn = vn.reshape(b, n // GM_CHUNK, GM_CHUNK, GM_GROUPS, GM_GW)
    sv = jnp.einsum('gpq,bnqgc->bnpgc', w_s, vn) + jnp.transpose(b_s)[:, :, None]
    return u * sv.reshape(b, n, GM_W)


def _mixer(proj, q_nope, q_rope, k_nope, k_rope, v, w_pool, pool_scale, w_s, b_s, w_branch, w_out):
    b, n, _ = proj.shape
    attn = _block_attention(q_nope, q_rope, k_nope, k_rope, v)
    pool = _multiscale_pool(proj[..., POOL_START:POOL_END], w_pool, pool_scale)
    gm = _spatial_gating(proj[..., GM_START:GM_END], w_s, b_s)
    g = jax.nn.sigmoid(proj[..., GATE_START:].astype(jnp.float32)).astype(proj.dtype)
    g = g.reshape(b, n, N_BRANCH, D_MODEL)
    y = (g[:, :, 0] * (attn @ w_branch[:ATTN_W])
         + g[:, :, 1] * (pool @ w_branch[ATTN_W:ATTN_W + POOL_W])
         + g[:, :, 2] * (gm @ w_branch[ATTN_W + POOL_W:]))
    return y @ w_out


def _expert_choice_moe(h, w_router, w_gate, w_up, w_down):
    b, n, dm = h.shape
    cap = (CAPACITY_FACTOR * n) // N_EXPERTS
    aff = jax.nn.softmax((h @ w_router).astype(jnp.float32), axis=-1)
    gates, idx = lax.top_k(jnp.swapaxes(aff, 1, 2), cap)
    xs = jax.vmap(lambda hb, ib: hb[ib])(h, idx)
    a = jnp.einsum('becd,edf->becf', xs, w_gate)
    u = jnp.einsum('becd,edf->becf', xs, w_up)
    y = jnp.einsum('becf,efd->becd', jax.nn.silu(a) * u, w_down) * gates[..., None].astype(h.dtype)
    return jax.vmap(lambda yb, ib: jnp.zeros((n, dm), yb.dtype).at[ib.reshape(-1)].add(yb.reshape(-1, dm)))(y, idx)


def _nrm(k, shape, scale):
    return jax.random.normal(k, shape, jnp.float32) * scale


def setup_inputs(seed: int = 0) -> dict:
    key = jax.random.key(seed)
    ks = jax.random.split(key, 24)
    L = DEPTH
    return {
        'x': _nrm(ks[0], (BATCH, SEQ, D_MODEL), 1.0),
        'c': _nrm(ks[1], (BATCH, D_MODEL), 1.0),
        'ctx': _nrm(ks[2], (BATCH, CTX_LEN, D_MODEL), 1.0),
        'c_ctx': _nrm(ks[3], (D_MODEL,), 1.0),
        'w_ada': _nrm(ks[4], (L, D_MODEL, 6 * D_MODEL), 0.5 * D_MODEL ** -0.5),
        'b_ada': _nrm(ks[5], (L, 6 * D_MODEL), 0.02),
        'w_in': _nrm(ks[6], (L, D_MODEL, IN_COLS), D_MODEL ** -0.5),
        'q_norm_g': 1.0 + _nrm(ks[7], (L, Q_LORA), 0.02),
        'w_uq': _nrm(ks[8], (L, Q_LORA, N_HEADS * (QK_NOPE + QK_ROPE)), Q_LORA ** -0.5),
        'kv_norm_g': 1.0 + _nrm(ks[9], (L, KV_LORA), 0.02),
        'w_ukv': _nrm(ks[10], (L, KV_LORA, N_HEADS * (QK_NOPE + V_DIM)), KV_LORA ** -0.5),
        'w_pool': _nrm(ks[11], (L, len(POOL_WINDOWS), POOL_GW, POOL_GW), POOL_GW ** -0.5),
        'pool_scale': 1.0 + _nrm(ks[12], (L, POOL_W), 0.02),
        'w_spatial': _nrm(ks[13], (L, GM_GROUPS, GM_CHUNK, GM_CHUNK), GM_CHUNK ** -0.5),
        'b_spatial': 1.0 + _nrm(ks[14], (L, GM_GROUPS, GM_CHUNK), 0.02),
        'w_branch': _nrm(ks[15], (L, MIX_W, D_MODEL), POOL_W ** -0.5),
        'w_out': _nrm(ks[16], (L, D_MODEL, D_MODEL), D_MODEL ** -0.5),
        'w_router': _nrm(ks[17], (L, D_MODEL, N_EXPERTS), D_MODEL ** -0.5),
        'w_gate': _nrm(ks[18], (L, N_EXPERTS, D_MODEL, D_EXPERT), D_MODEL ** -0.5),
        'w_up': _nrm(ks[19], (L, N_EXPERTS, D_MODEL, D_EXPERT), D_MODEL ** -0.5),
        'w_down': _nrm(ks[20], (L, N_EXPERTS, D_EXPERT, D_MODEL), D_EXPERT ** -0.5),
        'final_g': 1.0 + _nrm(ks[21], (D_MODEL,), 0.02),
    }


def reference(x, c, ctx, c_ctx, w_ada, b_ada, w_in, q_norm_g, w_uq, kv_norm_g, w_ukv, w_pool, pool_scale,
              w_spatial, b_spatial, w_branch, w_out, w_router, w_gate, w_up, w_down, final_g):
    n = x.shape[1]
    tables = _axial_rope_tables(n)
    for l in range(DEPTH):
        last = l == DEPTH - 1
        m = jax.nn.silu(c) @ w_ada[l] + b_ada[l]
        sh1, sc1, g1, sh2, sc2, g2 = jnp.split(m[:, None, :], 6, axis=-1)
        mc = jax.nn.silu(c_ctx) @ w_ada[l] + b_ada[l]
        csh1, csc1, cg1, csh2, csc2, cg2 = jnp.split(mc[None, None, :], 6, axis=-1)

        px = _modulate(x, sh1, sc1) @ w_in[l]
        hc = _modulate(ctx, csh1, csc1)
        if last:
            pc_kv = hc @ w_in[l][:, KV_START:KV_END]
        else:
            pc = hc @ w_in[l]
            pc_kv = pc[..., KV_START:KV_END]
        ck_nope, ck_rope, cv = _mla_keys(pc_kv, kv_norm_g[l], w_ukv[l])
        xk_nope, xk_rope, xv = _mla_keys(px[..., KV_START:KV_END], kv_norm_g[l], w_ukv[l])
        xk_rope = _axial_rope(xk_rope[:, :, None, :], tables)[:, :, 0, :]
        xq_nope, xq_rope = _mla_queries(px, q_norm_g[l], w_uq[l])
        xq_rope = _axial_rope(xq_rope, tables)
        k_nope = jnp.concatenate([ck_nope, xk_nope], axis=1)
        k_rope = jnp.concatenate([ck_rope, xk_rope], axis=1)
        vv = jnp.concatenate([cv, xv], axis=1)
        x = x + g1 * _mixer(px, xq_nope, xq_rope, k_nope, k_rope, vv, w_pool[l], pool_scale[l],
                            w_spatial[l], b_spatial[l], w_branch[l], w_out[l])
        x = x + g2 * _expert_choice_moe(_modulate(x, sh2, sc2), w_router[l], w_gate[l], w_up[l], w_down[l])

        if not last:
            cq_nope, cq_rope = _mla_queries(pc, q_norm_g[l], w_uq[l])
            ctx = ctx + cg1 * _mixer(pc, cq_nope, cq_rope, ck_nope, ck_rope, cv, w_pool[l], pool_scale[l],
                                     w_spatial[l], b_spatial[l], w_branch[l], w_out[l])
            ctx = ctx + cg2 * _expert_choice_moe(_modulate(ctx, csh2, csc2), w_router[l], w_gate[l],
                                                 w_up[l], w_down[l])
    return _rms(x) * final_g
```

```python
import functools

import jax
import jax.numpy as jnp
from jax import lax
from jax.experimental import pallas as pl
from jax.experimental.pallas import tpu as pltpu

F32 = jnp.float32
BF16 = jnp.bfloat16
I32 = jnp.int32

GRID_W = 64
N_HEADS = 16
Q_LORA = 1024
KV_LORA = 512
QK_NOPE = 128
QK_ROPE = 64
V_DIM = 128
ROPE_BASE = 10000.0
POOL_WINDOWS = (2, 4, 8, 16)
POOL_W = 1024
GM_CHUNK = 128
GM_GROUPS = 4
GM_W = 1024
N_EXPERTS = 16
CAPACITY_FACTOR = 2
EPS = 1e-6

V7X_LANES = 128
V7X_VMEM_BYTES = 64 * 2**20
MIB = 2**20

ROW_TILE = 512
NORM_CHUNK = 16
POOL_HALO = 64
MOE_TILE = 256
DMA_WORDS = 128


def _params(semantics, vmem_mib):
    return pltpu.CompilerParams(dimension_semantics=semantics, vmem_limit_bytes=int(vmem_mib * MIB))


def _pick(n, pref):
    best = None
    for t in range(V7X_LANES, min(n, pref) + 1, V7X_LANES):
        if n % t == 0:
            best = t
    assert best is not None, (n, pref)
    return best


def _ada_kernel(c_ref, w_ref, b_ref, o_ref):
    c = c_ref[...]
    s = c * jax.nn.sigmoid(c)
    acc = jnp.dot(s, w_ref[...], preferred_element_type=F32, precision=lax.Precision.HIGHEST)
    o_ref[...] = acc + b_ref[...]


def _ada(cc, w_ada, b_ada):
    L, D, N = w_ada.shape
    R = cc.shape[0]
    tn = _pick(N, 512)
    return pl.pallas_call(
        _ada_kernel,
        out_shape=jax.ShapeDtypeStruct((L, R, N), F32),
        grid=(L, N // tn),
        in_specs=[
            pl.BlockSpec((R, D), lambda l, j: (0, 0)),
            pl.BlockSpec((None, D, tn), lambda l, j: (l, 0, j)),
            pl.BlockSpec((None, 1, tn), lambda l, j: (l, 0, j)),
        ],
        out_specs=pl.BlockSpec((None, R, tn), lambda l, j: (l, 0, j)),
        compiler_params=_params(("arbitrary", "arbitrary"), 2 * 2 * D * tn * 4 / MIB + 8),
        name="ada",
    )(cc, w_ada, b_ada.reshape(L, 1, N))


def _norm_rows(x_ref, a, b, dst_refs, rows):
    def body(c, carry):
        r = pl.multiple_of(c * NORM_CHUNK, NORM_CHUNK)
        x = x_ref[pl.ds(r, NORM_CHUNK), :].astype(F32)
        ms = jnp.mean(x * x, axis=-1, keepdims=True)
        h = x * lax.rsqrt(ms + EPS) * a + b
        for d in dst_refs:
            d[pl.ds(r, NORM_CHUNK), :] = h.astype(d.dtype)
        return carry

    lax.fori_loop(0, rows // NORM_CHUNK, body, 0)


def _affine(a_ref, b_ref, mode):
    a = a_ref[...]
    if mode == "mod":
        a = 1.0 + a
    return a, b_ref[...]


def _norm_mm_kernel(x_ref, a_ref, b_ref, w_ref, o_ref, h_ref, *, mode, act):
    @pl.when(pl.program_id(1) == 0)
    def _():
        a, b = _affine(a_ref, b_ref, mode)
        _norm_rows(x_ref, a, b, (h_ref,), x_ref.shape[0])

    acc = jnp.dot(h_ref[...], w_ref[...], preferred_element_type=F32)
    if act == "sigmoid":
        acc = jax.nn.sigmoid(acc)
    o_ref[...] = acc.astype(o_ref.dtype)


def _norm_mm(x, xcol, kx, a, b, w, *, mode, act, out_dtype, tm, tn, mod_fn, name):
    T = x.shape[0]
    N = w.shape[1]
    osz = jnp.dtype(out_dtype).itemsize
    vmem = (2 * tm * kx * 4 + tm * kx * 2 + 2 * kx * tn * 2 + 2 * tm * tn * osz) / MIB + 8
    return pl.pallas_call(
        functools.partial(_norm_mm_kernel, mode=mode, act=act),
        out_shape=jax.ShapeDtypeStruct((T, N), out_dtype),
        grid=(T // tm, N // tn),
        in_specs=[
            pl.BlockSpec((tm, kx), lambda i, j: (i, xcol)),
            pl.BlockSpec((None, 1, kx), lambda i, j: (mod_fn(i), 0, 0)),
            pl.BlockSpec((None, 1, kx), lambda i, j: (mod_fn(i), 0, 0)),
            pl.BlockSpec((kx, tn), lambda i, j: (0, j)),
        ],
        out_specs=pl.BlockSpec((tm, tn), lambda i, j: (i, j)),
        scratch_shapes=[pltpu.VMEM((tm, kx), BF16)],
        compiler_params=_params(("parallel", "arbitrary"), vmem),
        name=name,
    )(x, a, b, w)


def _rope128(t, cos, sin):
    return t * cos + pltpu.roll(t, QK_ROPE, 1) * sin


def _q_kernel(x_ref, a_ref, b_ref, w_ref, cos_ref, sin_ref, o_ref, h_ref, *, scale):
    @pl.when(pl.program_id(1) == 0)
    def _():
        _norm_rows(x_ref, a_ref[...], b_ref[...], (h_ref,), x_ref.shape[0])

    acc = jnp.dot(h_ref[...], w_ref[...], preferred_element_type=F32)
    rot = _rope128(acc[:, QK_NOPE:], cos_ref[...], sin_ref[...])
    o_ref[:, :QK_NOPE] = (acc[:, :QK_NOPE] * scale).astype(o_ref.dtype)
    o_ref[:, QK_NOPE:] = (rot * scale).astype(o_ref.dtype)


def _kv_kernel(x_ref, a_ref, b_ref, w_ref, kr_ref, cos_ref, sin_ref, k_ref, v_ref, h_ref):
    @pl.when(pl.program_id(1) == 0)
    def _():
        _norm_rows(x_ref, a_ref[...], b_ref[...], (h_ref,), x_ref.shape[0])

    acc = jnp.dot(h_ref[...], w_ref[...], preferred_element_type=F32)
    k_ref[:, :QK_NOPE] = acc[:, :QK_NOPE].astype(k_ref.dtype)
    k_ref[:, QK_NOPE:] = _rope128(kr_ref[...], cos_ref[...], sin_ref[...]).astype(k_ref.dtype)
    v_ref[...] = acc[:, QK_NOPE:].astype(v_ref.dtype)


def _attn_kernel(*refs, tk, n_lat):
    if n_lat:
        q_ref, kc_ref, vc_ref, kl_ref, vl_ref, o_ref = refs
    else:
        q_ref, kc_ref, vc_ref, o_ref = refs
    q = q_ref[...]

    def scores(k):
        return lax.dot_general(q, k, (((1,), (1,)), ((), ())), preferred_element_type=F32)

    s = scores(kc_ref[...])
    m = jnp.max(s, axis=-1, keepdims=True)
    p = jnp.exp(s - m)
    l = jnp.sum(p, axis=-1, keepdims=True)
    acc = jnp.dot(p.astype(BF16), vc_ref[...], preferred_element_type=F32)

    if n_lat:
        def body(c, carry):
            m, l, acc = carry
            r = pl.multiple_of(c * tk, tk)
            s = scores(kl_ref[pl.ds(r, tk), :])
            m_new = jnp.maximum(m, jnp.max(s, axis=-1, keepdims=True))
            alpha = jnp.exp(m - m_new)
            p = jnp.exp(s - m_new)
            l = alpha * l + jnp.sum(p, axis=-1, keepdims=True)
            acc = alpha * acc + jnp.dot(p.astype(BF16), vl_ref[pl.ds(r, tk), :], preferred_element_type=F32)
            return m_new, l, acc

        m, l, acc = lax.fori_loop(0, n_lat, body, (m, l, acc))
    o_ref[...] = (acc / l).astype(o_ref.dtype)


def _attention(qc, kc, vc, *, B, SEQ, CTX):
    H, T, dq = qc.shape
    NL = B * SEQ
    tq = min(512, SEQ)
    tk = min(512, SEQ)
    nq = SEQ // tq
    cblk = NL // CTX
    lat = pl.pallas_call(
        functools.partial(_attn_kernel, tk=tk, n_lat=SEQ // tk),
        out_shape=jax.ShapeDtypeStruct((NL, H * V_DIM), BF16),
        grid=(B, H, nq),
        in_specs=[
            pl.BlockSpec((None, tq, dq), lambda b, h, i: (h, b * nq + i, 0)),
            pl.BlockSpec((None, CTX, dq), lambda b, h, i: (h, cblk + b, 0)),
            pl.BlockSpec((None, CTX, V_DIM), lambda b, h, i: (h, cblk + b, 0)),
            pl.BlockSpec((None, SEQ, dq), lambda b, h, i: (h, b, 0)),
            pl.BlockSpec((None, SEQ, V_DIM), lambda b, h, i: (h, b, 0)),
        ],
        out_specs=pl.BlockSpec((tq, V_DIM), lambda b, h, i: (b * nq + i, h)),
        compiler_params=_params(("parallel", "parallel", "arbitrary"),
                                2 * SEQ * (dq + V_DIM) * 2 / MIB + 24),
        name="attn_latent",
    )(qc, kc, vc, kc, vc)
    ctx = pl.pallas_call(
        functools.partial(_attn_kernel, tk=tk, n_lat=0),
        out_shape=jax.ShapeDtypeStruct((B * CTX, H * V_DIM), BF16),
        grid=(B, H),
        in_specs=[
            pl.BlockSpec((None, CTX, dq), lambda b, h: (h, cblk + b, 0)),
            pl.BlockSpec((None, CTX, dq), lambda b, h: (h, cblk + b, 0)),
            pl.BlockSpec((None, CTX, V_DIM), lambda b, h: (h, cblk + b, 0)),
        ],
        out_specs=pl.BlockSpec((CTX, V_DIM), lambda b, h: (b, h)),
        compiler_params=_params(("parallel", "arbitrary"), 16),
        name="attn_ctx",
    )(qc, kc, vc)
    return jnp.concatenate([lat, ctx], axis=0)


def _pool_kernel(prev_ref, x_ref, next_ref, w_ref, ps_ref, o_ref, *, NL, SEQ, CTX):
    tm, width = x_ref.shape
    gw = width // len(POOL_WINDOWS)
    row0 = pl.program_id(0) * tm
    xm = x_ref[...]
    xcat = jnp.concatenate([prev_ref[...], xm, next_ref[...]], axis=0)
    hi_part = xcat.astype(BF16)
    lo_part = (xcat - hi_part.astype(F32)).astype(BF16)

    r = row0 + lax.broadcasted_iota(I32, (tm, 1), 0)
    s = row0 - POOL_HALO + lax.broadcasted_iota(I32, (1, tm + 2 * POOL_HALO), 1)
    lat_lo = (row0 // SEQ) * SEQ
    ctx_lo = jnp.full((tm, 1), row0, I32)
    for k in range(1, max(tm // CTX, 1)):
        ctx_lo = jnp.where(r >= row0 + k * CTX, row0 + k * CTX, ctx_lo)
    is_lat = row0 < NL
    seq_lo = jnp.where(is_lat, lat_lo, ctx_lo)
    seq_hi = seq_lo + jnp.where(is_lat, SEQ, CTX)

    for g, win in enumerate(POOL_WINDOWS):
        left = win // 2
        right = win - 1 - left
        lo = jnp.maximum(r - left, seq_lo)
        hi = jnp.minimum(r + right + 1, seq_hi)
        band = jnp.where((s >= lo) & (s < hi), 1.0, 0.0).astype(BF16)
        cols = slice(g * gw, (g + 1) * gw)
        tot = (jnp.dot(band, hi_part[:, cols], preferred_element_type=F32)
               + jnp.dot(band, lo_part[:, cols], preferred_element_type=F32))
        d = tot / (hi - lo).astype(F32) - xm[:, cols]
        y = jnp.dot(d.astype(BF16), w_ref[g], preferred_element_type=F32)
        o_ref[:, cols] = (y * ps_ref[:, cols]).astype(o_ref.dtype)


def _pool(pa, col, w_pool, pool_scale, *, tm, NL, SEQ, CTX):
    T = pa.shape[0]
    hb = tm // POOL_HALO
    last = T // POOL_HALO - 1
    return pl.pallas_call(
        functools.partial(_pool_kernel, NL=NL, SEQ=SEQ, CTX=CTX),
        out_shape=jax.ShapeDtypeStruct((T, POOL_W), BF16),
        grid=(T // tm,),
        in_specs=[
            pl.BlockSpec((POOL_HALO, POOL_W), lambda i: (jnp.maximum(i * hb - 1, 0), col)),
            pl.BlockSpec((tm, POOL_W), lambda i: (i, col)),
            pl.BlockSpec((POOL_HALO, POOL_W), lambda i: (jnp.minimum((i + 1) * hb, last), col)),
            pl.BlockSpec(w_pool.shape, lambda i: (0, 0, 0)),
            pl.BlockSpec((1, POOL_W), lambda i: (0, 0)),
        ],
        out_specs=pl.BlockSpec((tm, POOL_W), lambda i: (i, 0)),
        compiler_params=_params(("parallel",), 40),
        name="pool",
    )(pa, pa, pa, w_pool, pool_scale)


def _gelu(z):
    return 0.5 * z * (1.0 + lax.erf(z * (2.0 ** -0.5)))


def _gm_kernel(u_ref, v_ref, ws_ref, bias_ref, o_ref):
    tm = u_ref.shape[0]
    gw = GM_W // GM_GROUPS

    def body(c, carry):
        r = pl.multiple_of(c * GM_CHUNK, GM_CHUNK)
        zu = _gelu(u_ref[pl.ds(r, GM_CHUNK), :])
        zv = _gelu(v_ref[pl.ds(r, GM_CHUNK), :])
        mu = jnp.mean(zv, axis=-1, keepdims=True)
        var = jnp.mean(jnp.square(zv - mu), axis=-1, keepdims=True)
        vn = ((zv - mu) * lax.rsqrt(var + EPS)).astype(BF16)
        for g in range(GM_GROUPS):
            cols = slice(g * gw, (g + 1) * gw)
            sv = jnp.dot(ws_ref[g], vn[:, cols], preferred_element_type=F32) + bias_ref[:, cols]
            o_ref[pl.ds(r, GM_CHUNK), cols] = (zu[:, cols] * sv).astype(o_ref.dtype)
        return carry

    lax.fori_loop(0, tm // GM_CHUNK, body, 0)


def _gm(pa, ucol, w_s, bias_full, *, tm):
    T = pa.shape[0]
    return pl.pallas_call(
        _gm_kernel,
        out_shape=jax.ShapeDtypeStruct((T, GM_W), BF16),
        grid=(T // tm,),
        in_specs=[
            pl.BlockSpec((tm, GM_W), lambda i: (i, ucol)),
            pl.BlockSpec((tm, GM_W), lambda i: (i, ucol + 1)),
            pl.BlockSpec(w_s.shape, lambda i: (0, 0, 0)),
            pl.BlockSpec(bias_full.shape, lambda i: (0, 0)),
        ],
        out_specs=pl.BlockSpec((tm, GM_W), lambda i: (i, 0)),
        compiler_params=_params(("parallel",), 32),
        name="gmlp",
    )(pa, pa, w_s, bias_full)


def _branch_kernel(at_ref, po_ref, gm_ref, w_ref, g0_ref, g1_ref, g2_ref, o_ref):
    ka = at_ref.shape[1]
    kp = po_ref.shape[1]
    y = g0_ref[...].astype(F32) * jnp.dot(at_ref[...], w_ref[0:ka, :], preferred_element_type=F32)
    y += g1_ref[...].astype(F32) * jnp.dot(po_ref[...], w_ref[ka:ka + kp, :], preferred_element_type=F32)
    y += g2_ref[...].astype(F32) * jnp.dot(gm_ref[...], w_ref[ka + kp:, :], preferred_element_type=F32)
    o_ref[...] = y.astype(o_ref.dtype)


def _branch(attn, pool, gm, w_branch, gates, *, tm, tn):
    T = attn.shape[0]
    K, D = w_branch.shape
    nj = D // tn
    return pl.pallas_call(
        _branch_kernel,
        out_shape=jax.ShapeDtypeStruct((T, D), BF16),
        grid=(T // tm, nj),
        in_specs=[
            pl.BlockSpec((tm, attn.shape[1]), lambda i, j: (i, 0)),
            pl.BlockSpec((tm, pool.shape[1]), lambda i, j: (i, 0)),
            pl.BlockSpec((tm, gm.shape[1]), lambda i, j: (i, 0)),
            pl.BlockSpec((K, tn), lambda i, j: (0, j)),
            pl.BlockSpec((tm, tn), lambda i, j: (i, j)),
            pl.BlockSpec((tm, tn), lambda i, j: (i, nj + j)),
            pl.BlockSpec((tm, tn), lambda i, j: (i, 2 * nj + j)),
        ],
        out_specs=pl.BlockSpec((tm, tn), lambda i, j: (i, j)),
        compiler_params=_params(("parallel", "arbitrary"),
                                (2 * tm * K * 2 + 2 * K * tn * 2 + 8 * tm * tn * 2 + 4 * tm * tn * 4) / MIB + 8),
        name="branch_mix",
    )(attn, pool, gm, w_branch, gates, gates, gates)


def _mm_res_kernel(a_ref, w_ref, r_ref, g_ref, o_ref):
    acc = jnp.dot(a_ref[...], w_ref[...], preferred_element_type=F32)
    o_ref[...] = r_ref[...] + g_ref[...] * acc


def _mm_res(a, w, res, gate, *, tm, tn, mod_fn):
    T, K = a.shape
    N = w.shape[1]
    return pl.pallas_call(
        _mm_res_kernel,
        out_shape=jax.ShapeDtypeStruct((T, N), F32),
        grid=(T // tm, N // tn),
        in_specs=[
            pl.BlockSpec((tm, K), lambda i, j: (i, 0)),
            pl.BlockSpec((K, tn), lambda i, j: (0, j)),
            pl.BlockSpec((tm, tn), lambda i, j: (i, j)),
            pl.BlockSpec((None, 1, tn), lambda i, j: (mod_fn(i), 0, j)),
        ],
        out_specs=pl.BlockSpec((tm, tn), lambda i, j: (i, j)),
        input_output_aliases={2: 0},
        compiler_params=_params(("parallel", "arbitrary"),
                                (2 * tm * K * 2 + 2 * K * tn * 2 + 6 * tm * tn * 4) / MIB + 8),
        name="out_proj",
    )(a, w, res, gate)


def _mod_router_kernel(x_ref, sc_ref, sh_ref, wr_ref, h_ref, lg_ref, hb_ref):
    a, b = _affine(sc_ref, sh_ref, "mod")
    _norm_rows(x_ref, a, b, (h_ref, hb_ref), x_ref.shape[0])
    lg_ref[...] = lax.dot_general(wr_ref[...], hb_ref[...], (((1,), (1,)), ((), ())),
                                  preferred_element_type=F32)


def _mod_router(xs, sc, sh, wr_t, *, tm, mod_fn):
    T, D = xs.shape
    E = wr_t.shape[0]
    return pl.pallas_call(
        _mod_router_kernel,
        out_shape=(jax.ShapeDtypeStruct((T, D), F32), jax.ShapeDtypeStruct((E, T), F32)),
        grid=(T // tm,),
        in_specs=[
            pl.BlockSpec((tm, D), lambda i: (i, 0)),
            pl.BlockSpec((None, 1, D), lambda i: (mod_fn(i), 0, 0)),
            pl.BlockSpec((None, 1, D), lambda i: (mod_fn(i), 0, 0)),
            pl.BlockSpec((E, D), lambda i: (0, 0)),
        ],
        out_specs=(pl.BlockSpec((tm, D), lambda i: (i, 0)), pl.BlockSpec((E, tm), lambda i: (0, i))),
        scratch_shapes=[pltpu.VMEM((tm, D), BF16)],
        compiler_params=_params(("parallel",), (4 * tm * D * 4 + tm * D * 2) / MIB + 8),
        name="mod_router",
    )(xs, sc, sh, wr_t)


def _route_kernel(lg_ref, code_ref, *, cap):
    lg = lg_ref[...]
    E, n = lg.shape
    mx = jnp.max(lg, axis=0, keepdims=True)
    ex = jnp.exp(lg - mx)
    aff = ex / jnp.sum(ex, axis=0, keepdims=True)
    bits = pltpu.bitcast(aff, I32)

    def count(mask):
        return jnp.sum(jnp.where(mask, 1.0, 0.0), axis=1, keepdims=True)

    v = jnp.zeros((E, 1), I32)
    for bit in range(30, -1, -1):
        cand = v | (1 << bit)
        v = jnp.where(count(bits >= cand) >= cap, cand, v)
    gt = bits > v
    eq = bits == v
    need = cap - count(gt)
    t = lax.broadcasted_iota(I32, (E, n), 1)
    u = jnp.zeros((E, 1), I32)
    for bit in range(max(n - 1, 1).bit_length() - 1, -1, -1):
        cand = u | (1 << bit)
        u = jnp.where(count(eq & (t < cand)) < need, cand, u)
    sel = gt | (eq & (t <= u))
    code_ref[...] = jnp.where(sel, bits | jnp.int32(-2**31), bits)


def _route(logits_t, *, n, cap, blk0, sets):
    E = logits_t.shape[0]
    return pl.pallas_call(
        functools.partial(_route_kernel, cap=cap),
        out_shape=jax.ShapeDtypeStruct((sets, E, n), I32),
        grid=(sets,),
        in_specs=[pl.BlockSpec((E, n), lambda s: (0, blk0 + s))],
        out_specs=pl.BlockSpec((None, E, n), lambda s: (s, 0, 0)),
        compiler_params=_params(("parallel",), 32),
        name="route_select",
    )(logits_t)


def _compact_kernel(code_hbm, idx_hbm, gb_hbm, code_s, idx_s, gb_s, sem, *, n, cap):
    s = pl.program_id(0)
    e = pl.program_id(1)
    cin = pltpu.make_async_copy(code_hbm.at[s, e], code_s, sem)
    cin.start()
    cin.wait()

    def body(t, c):
        w = code_s[t]
        idx_s[c] = t
        gb_s[c] = w & 0x7FFFFFFF
        return c + lax.shift_right_logical(w, jnp.int32(31))

    lax.fori_loop(0, n, body, jnp.int32(0), unroll=8)
    cap_pad = idx_hbm.shape[-1]

    def clear(p, carry):
        idx_s[p] = 0
        gb_s[p] = 0
        return carry

    lax.fori_loop(cap, cap_pad, clear, 0)
    o1 = pltpu.make_async_copy(idx_s.at[pl.ds(0, cap_pad)], idx_hbm.at[s, e], sem)
    o1.start()
    o1.wait()
    o2 = pltpu.make_async_copy(gb_s.at[pl.ds(0, cap_pad)], gb_hbm.at[s, e], sem)
    o2.start()
    o2.wait()


def _compact(code, *, cap):
    sets, E, n = code.shape
    cap_pad = -(-cap // DMA_WORDS) * DMA_WORDS
    idx, gbits = pl.pallas_call(
        functools.partial(_compact_kernel, n=n, cap=cap),
        out_shape=(jax.ShapeDtypeStruct((sets, E, cap_pad), I32), jax.ShapeDtypeStruct((sets, E, cap_pad), I32)),
        grid=(sets, E),
        in_specs=[pl.BlockSpec(memory_space=pl.ANY)],
        out_specs=(pl.BlockSpec(memory_space=pl.ANY), pl.BlockSpec(memory_space=pl.ANY)),
        scratch_shapes=[pltpu.SMEM((n,), I32), pltpu.SMEM((cap_pad + DMA_WORDS,), I32),
                        pltpu.SMEM((cap_pad + DMA_WORDS,), I32), pltpu.SemaphoreType.DMA(())],
        compiler_params=_params(("arbitrary", "arbitrary"), 16),
        name="route_compact",
    )(code)
    return idx[:, :, :cap], gbits[:, :, :cap]


def _row_copies(src_hbm, dst_vmem, idx_s, sem, rows, *, to_hbm=False):
    def body(r, carry):
        row = idx_s[0, r]
        if to_hbm:
            pltpu.make_async_copy(dst_vmem.at[pl.ds(r, 1)], src_hbm.at[pl.ds(row, 1)], sem).start()
        else:
            pltpu.make_async_copy(src_hbm.at[pl.ds(row, 1)], dst_vmem.at[pl.ds(r, 1)], sem).start()
        return carry

    lax.fori_loop(0, rows, body, 0, unroll=8)


def _row_waits(src_hbm, dst_vmem, sem, rows, *, to_hbm=False):
    def body(r, carry):
        if to_hbm:
            pltpu.make_async_copy(dst_vmem.at[pl.ds(r, 1)], src_hbm.at[pl.ds(0, 1)], sem).wait()
        else:
            pltpu.make_async_copy(src_hbm.at[pl.ds(0, 1)], dst_vmem.at[pl.ds(r, 1)], sem).wait()
        return carry

    lax.fori_loop(0, rows, body, 0, unroll=8)


def _load_idx(idx_hbm, idx_s, sem, step):
    cp = pltpu.make_async_copy(idx_hbm.at[pl.ds(step, 1)], idx_s, sem)
    cp.start()
    cp.wait()


def _ffn1_kernel(idx_hbm, h_hbm, wg_ref, wu_ref, o_ref, xbuf, idx_s, sem_i, sem_g):
    tm = xbuf.shape[0]
    step = pl.program_id(0) * pl.num_programs(1) + pl.program_id(1)
    _load_idx(idx_hbm, idx_s, sem_i, step)
    _row_copies(h_hbm, xbuf, idx_s, sem_g, tm)
    _row_waits(h_hbm, xbuf, sem_g, tm)
    xb = xbuf[...].astype(BF16)
    a = jnp.dot(xb, wg_ref[...], preferred_element_type=F32)
    u = jnp.dot(xb, wu_ref[...], preferred_element_type=F32)
    o_ref[...] = (a * jax.nn.sigmoid(a) * u).astype(o_ref.dtype)


def _ffn1(idx_tiles, h2, w_gate, w_up, *, tm):
    E, D, F = w_gate.shape
    tiles = idx_tiles.shape[0] // E
    return pl.pallas_call(
        _ffn1_kernel,
        out_shape=jax.ShapeDtypeStruct((E, tiles * tm, F), BF16),
        grid=(E, tiles),
        in_specs=[
            pl.BlockSpec(memory_space=pl.ANY),
            pl.BlockSpec(memory_space=pl.ANY),
            pl.BlockSpec((None, D, F), lambda e, m: (e, 0, 0)),
            pl.BlockSpec((None, D, F), lambda e, m: (e, 0, 0)),
        ],
        out_specs=pl.BlockSpec((None, tm, F), lambda e, m: (e, m, 0)),
        scratch_shapes=[pltpu.VMEM((tm, D), F32), pltpu.SMEM((1, max(tm, DMA_WORDS)), I32),
                        pltpu.SemaphoreType.DMA(()), pltpu.SemaphoreType.DMA(())],
        compiler_params=_params(("arbitrary", "arbitrary"),
                                (4 * D * F * 2 + tm * D * 6 + 4 * tm * F * 4) / MIB + 8),
        name="expert_ffn_in",
    )(idx_tiles, h2, w_gate, w_up)


def _ffn2_kernel(idx_hbm, hh_ref, wd_ref, gate_ref, g2_ref, acc_in, o_hbm, abuf, idx_s, sem_i, sem_g, sem_s):
    del acc_in
    tm = abuf.shape[0]
    step = pl.program_id(0) * pl.num_programs(1) + pl.program_id(1)
    _load_idx(idx_hbm, idx_s, sem_i, step)
    _row_copies(o_hbm, abuf, idx_s, sem_g, tm)
    y = jnp.dot(hh_ref[...], wd_ref[...], preferred_element_type=F32)
    y = y * gate_ref[...] * g2_ref[...]
    _row_waits(o_hbm, abuf, sem_g, tm)
    abuf[...] += y
    _row_copies(o_hbm, abuf, idx_s, sem_s, tm, to_hbm=True)
    _row_waits(o_hbm, abuf, sem_s, tm, to_hbm=True)


def _ffn2_scatter(idx_tiles, hh, w_down, gates, g2, xs, *, tm, mod_fn):
    E, F, D = w_down.shape
    tiles = idx_tiles.shape[0] // E
    return pl.pallas_call(
        _ffn2_kernel,
        out_shape=jax.ShapeDtypeStruct(xs.shape, xs.dtype),
        grid=(E, tiles),
        in_specs=[
            pl.BlockSpec(memory_space=pl.ANY),
            pl.BlockSpec((None, tm, F), lambda e, m: (e, m, 0)),
            pl.BlockSpec((None, F, D), lambda e, m: (e, 0, 0)),
            pl.BlockSpec((None, tm, 1), lambda e, m: (e, m, 0)),
            pl.BlockSpec((None, 1, D), lambda e, m: (mod_fn(m), 0, 0)),
            pl.BlockSpec(memory_space=pl.ANY),
        ],
        out_specs=pl.BlockSpec(memory_space=pl.ANY),
        scratch_shapes=[pltpu.VMEM((tm, D), F32), pltpu.SMEM((1, max(tm, DMA_WORDS)), I32),
                        pltpu.SemaphoreType.DMA(()), pltpu.SemaphoreType.DMA(()), pltpu.SemaphoreType.DMA(())],
        input_output_aliases={5: 0},
        compiler_params=_params(("arbitrary", "arbitrary"),
                                (2 * F * D * 2 + 3 * tm * D * 4 + 4 * tm * F * 2) / MIB + 8),
        name="expert_ffn_out",
    )(idx_tiles, hh, w_down, gates, g2, xs)


def _moe(xs, logits_t, h2, g2, w_gate, w_up, w_down, *, n, sets, row0, mod_row):
    E = logits_t.shape[0]
    cap = (CAPACITY_FACTOR * n) // E
    code = _route(logits_t, n=n, cap=cap, blk0=row0 // n, sets=sets)
    idx, gbits = _compact(code, cap=cap)
    tm = min(MOE_TILE, cap if mod_row is None else sets * cap)
    assert (sets * cap) % tm == 0 and (mod_row is not None or cap % tm == 0)
    rows = idx + (row0 + n * jnp.arange(sets, dtype=I32))[:, None, None]
    idx_tiles = jnp.transpose(rows, (1, 0, 2)).reshape(E * sets * cap // tm, tm)
    idx_tiles = jnp.pad(idx_tiles, ((0, 0), (0, max(DMA_WORDS - tm, 0))))
    gates = lax.bitcast_convert_type(jnp.transpose(gbits, (1, 0, 2)), F32).reshape(E, sets * cap, 1)
    hh = _ffn1(idx_tiles, h2, w_gate, w_up, tm=tm)
    mod_fn = (lambda m: m // (cap // tm)) if mod_row is None else (lambda m: mod_row)
    return _ffn2_scatter(idx_tiles, hh, w_down, gates, g2, xs, tm=tm, mod_fn=mod_fn)


def _final_norm_kernel(x_ref, g_ref, z_ref, o_ref):
    _norm_rows(x_ref, g_ref[...], z_ref[...], (o_ref,), x_ref.shape[0])


def _final_norm(xs, gain, rows, *, tm):
    D = xs.shape[1]
    return pl.pallas_call(
        _final_norm_kernel,
        out_shape=jax.ShapeDtypeStruct((rows, D), F32),
        grid=(rows // tm,),
        in_specs=[
            pl.BlockSpec((tm, D), lambda i: (i, 0)),
            pl.BlockSpec((1, D), lambda i: (0, 0)),
            pl.BlockSpec((1, D), lambda i: (0, 0)),
        ],
        out_specs=pl.BlockSpec((tm, D), lambda i: (i, 0)),
        compiler_params=_params(("parallel",), 4 * tm * D * 4 / MIB + 8),
        name="final_norm",
    )(xs, gain.reshape(1, D), jnp.zeros((1, D), F32))


def _rope_tables(B, SEQ, CTX):
    rows = SEQ // GRID_W
    row = jnp.broadcast_to(jnp.arange(rows, dtype=F32)[:, None], (rows, GRID_W)).reshape(-1)
    col = jnp.broadcast_to(jnp.arange(GRID_W, dtype=F32)[None, :], (rows, GRID_W)).reshape(-1)
    n_freq = QK_ROPE // 4
    inv = ROPE_BASE ** (-jnp.arange(n_freq, dtype=F32) / n_freq)
    ar = row[:, None] * inv
    ac = col[:, None] * inv
    cos = jnp.concatenate([jnp.cos(ar), jnp.cos(ar), jnp.cos(ac), jnp.cos(ac)], axis=-1)
    sin = jnp.concatenate([-jnp.sin(ar), jnp.sin(ar), -jnp.sin(ac), jnp.sin(ac)], axis=-1)
    pad = jnp.zeros((SEQ, 2 * QK_ROPE - QK_ROPE), F32)
    cos_l = jnp.concatenate([cos, pad], axis=-1)
    sin_l = jnp.concatenate([sin, pad], axis=-1)
    cos_c = jnp.concatenate([jnp.ones((B * CTX, QK_ROPE), F32), jnp.zeros((B * CTX, QK_ROPE), F32)], axis=-1)
    cos_t = jnp.concatenate([jnp.tile(cos_l, (B, 1)), cos_c], axis=0)
    sin_t = jnp.concatenate([jnp.tile(sin_l, (B, 1)), jnp.zeros((B * CTX, 2 * QK_ROPE), F32)], axis=0)
    return cos_t, sin_t


def _swap_perm():
    q = QK_ROPE // 4
    return jnp.concatenate([jnp.arange(q, 2 * q), jnp.arange(0, q), jnp.arange(3 * q, 4 * q), jnp.arange(2 * q, 3 * q)])


def kernel(x, c, ctx, c_ctx, w_ada, b_ada, w_in, q_norm_g, w_uq, kv_norm_g, w_ukv, w_pool, pool_scale,
           w_spatial, b_spatial, w_branch, w_out, w_router, w_gate, w_up, w_down, final_g):
    B, SEQ, D = x.shape
    CTX = ctx.shape[1]
    L = w_ada.shape[0]
    H = N_HEADS
    NL, NC = B * SEQ, B * CTX
    T = NL + NC
    tm = min(ROW_TILE, NC)
    assert SEQ % tm == 0 and NC % tm == 0 and tm % CTX == 0 and CTX % GM_CHUNK == 0 and SEQ % GM_CHUNK == 0
    assert QK_NOPE == V7X_LANES and 2 * QK_ROPE == V7X_LANES and V_DIM == V7X_LANES
    assert Q_LORA == POOL_W == GM_W and Q_LORA % KV_LORA == 0 and KV_LORA % V7X_LANES == 0
    tiles_per_seq = SEQ // tm

    def mod_tile(i):
        return jnp.minimum(i // tiles_per_seq, B)

    kv_start = Q_LORA
    kr_start = kv_start + KV_LORA
    pool_start = kr_start + QK_ROPE
    gm_start = pool_start + POOL_W
    gate_start = gm_start + 2 * GM_W
    wa_cols = Q_LORA + POOL_W + 2 * GM_W + KV_LORA + 2 * QK_ROPE
    tn_a = 512 if wa_cols > 512 else V7X_LANES
    wa_pad = -wa_cols % tn_a
    perm = _swap_perm()
    dh = QK_NOPE + QK_ROPE
    scale = float(dh) ** -0.5

    cos_t, sin_t = _rope_tables(B, SEQ, CTX)
    xs = jnp.concatenate([x.reshape(NL, D), ctx.reshape(NC, D)], axis=0)

    cc = jnp.zeros((8, D), F32).at[:B].set(c).at[B].set(c_ctx)
    mods = _ada(cc, w_ada, b_ada)[:, :B + 1].reshape(L, B + 1, 6, 1, D)

    zeros_q = jnp.zeros((1, 1, Q_LORA), F32)
    zeros_kv = jnp.zeros((1, 1, KV_LORA), F32)

    for l in range(L):
        sh1, sc1, g1, sh2, sc2, g2 = (mods[l, :, k] for k in range(6))
        wl = w_in[l]
        kr_w = wl[:, kr_start:kr_start + QK_ROPE]
        w_a = jnp.concatenate(
            [wl[:, :Q_LORA], wl[:, pool_start:pool_start + POOL_W], wl[:, gm_start:gm_start + 2 * GM_W],
             wl[:, kv_start:kv_start + KV_LORA], kr_w, kr_w[:, perm], jnp.zeros((D, wa_pad), F32)],
            axis=1).astype(BF16)
        w_g = wl[:, gate_start:].astype(BF16)
        uq = w_uq[l].reshape(Q_LORA, H, dh)
        uq_ext = jnp.concatenate([uq, uq[:, :, QK_NOPE:][:, :, perm]], axis=-1).reshape(Q_LORA, H * (dh + QK_ROPE))
        uq_ext = uq_ext.astype(BF16)
        ukv = w_ukv[l].astype(BF16)

        pa = _norm_mm(xs, 0, D, sc1, sh1, w_a, mode="mod", act=None, out_dtype=F32, tm=tm, tn=tn_a,
                      mod_fn=mod_tile, name="in_proj")
        gates = _norm_mm(xs, 0, D, sc1, sh1, w_g, mode="mod", act="sigmoid", out_dtype=BF16, tm=tm,
                         tn=_pick(3 * D, 512), mod_fn=mod_tile, name="gate_proj")
        col_pool = Q_LORA // POOL_W
        col_gm = (Q_LORA + POOL_W) // GM_W
        col_kv = (Q_LORA + POOL_W + 2 * GM_W) // KV_LORA
        col_kr = (Q_LORA + POOL_W + 2 * GM_W + KV_LORA) // V7X_LANES

        hq = dh + QK_ROPE
        qc = pl.pallas_call(
            functools.partial(_q_kernel, scale=scale),
            out_shape=jax.ShapeDtypeStruct((H, T, hq), BF16),
            grid=(T // tm, H),
            in_specs=[
                pl.BlockSpec((tm, Q_LORA), lambda i, h: (i, 0)),
                pl.BlockSpec((None, 1, Q_LORA), lambda i, h: (0, 0, 0)),
                pl.BlockSpec((None, 1, Q_LORA), lambda i, h: (0, 0, 0)),
                pl.BlockSpec((Q_LORA, hq), lambda i, h: (0, h)),
                pl.BlockSpec((tm, 2 * QK_ROPE), lambda i, h: (i, 0)),
                pl.BlockSpec((tm, 2 * QK_ROPE), lambda i, h: (i, 0)),
            ],
            out_specs=pl.BlockSpec((None, tm, hq), lambda i, h: (h, i, 0)),
            scratch_shapes=[pltpu.VMEM((tm, Q_LORA), BF16)],
            compiler_params=_params(("parallel", "arbitrary"), 24),
            name="q_proj",
        )(pa, q_norm_g[l].reshape(1, 1, Q_LORA), zeros_q, uq_ext, cos_t, sin_t)
        kc, vc = pl.pallas_call(
            _kv_kernel,
            out_shape=(jax.ShapeDtypeStruct((H, T, hq), BF16), jax.ShapeDtypeStruct((H, T, V_DIM), BF16)),
            grid=(T // tm, H),
            in_specs=[
                pl.BlockSpec((tm, KV_LORA), lambda i, h: (i, col_kv)),
                pl.BlockSpec((None, 1, KV_LORA), lambda i, h: (0, 0, 0)),
                pl.BlockSpec((None, 1, KV_LORA), lambda i, h: (0, 0, 0)),
                pl.BlockSpec((KV_LORA, QK_NOPE + V_DIM), lambda i, h: (0, h)),
                pl.BlockSpec((tm, 2 * QK_ROPE), lambda i, h: (i, col_kr)),
                pl.BlockSpec((tm, 2 * QK_ROPE), lambda i, h: (i, 0)),
                pl.BlockSpec((tm, 2 * QK_ROPE), lambda i, h: (i, 0)),
            ],
            out_specs=(pl.BlockSpec((None, tm, hq), lambda i, h: (h, i, 0)),
                       pl.BlockSpec((None, tm, V_DIM), lambda i, h: (h, i, 0))),
            scratch_shapes=[pltpu.VMEM((tm, KV_LORA), BF16)],
            compiler_params=_params(("parallel", "arbitrary"), 24),
            name="kv_proj",
        )(pa, kv_norm_g[l].reshape(1, 1, KV_LORA), zeros_kv, ukv, pa, cos_t, sin_t)
        attn = _attention(qc, kc, vc, B=B, SEQ=SEQ, CTX=CTX)

        pool = _pool(pa, col_pool, w_pool[l].astype(BF16), pool_scale[l].reshape(1, POOL_W),
                     tm=tm, NL=NL, SEQ=SEQ, CTX=CTX)
        bias_full = jnp.repeat(jnp.transpose(b_spatial[l]), GM_W // GM_GROUPS, axis=1)
        gm = _gm(pa, col_gm, w_spatial[l].astype(BF16), bias_full, tm=tm)

        y = _branch(attn, pool, gm, w_branch[l].astype(BF16), gates, tm=tm, tn=_pick(D, 512))
        xs = _mm_res(y, w_out[l].astype(BF16), xs, g1, tm=tm, tn=_pick(D, 512), mod_fn=mod_tile)

        h2, logits_t = _mod_router(xs, sc2, sh2, jnp.transpose(w_router[l]).astype(BF16), tm=tm, mod_fn=mod_tile)
        wg, wu, wd = w_gate[l].astype(BF16), w_up[l].astype(BF16), w_down[l].astype(BF16)
        xs = _moe(xs, logits_t, h2, g2, wg, wu, wd, n=SEQ, sets=B, row0=0, mod_row=None)
        if l != L - 1:
            xs = _moe(xs, logits_t, h2, g2, wg, wu, wd, n=CTX, sets=B, row0=NL, mod_row=B)

    return _final_norm(xs, final_g, NL, tm=tm).reshape(B, SEQ, D)
```

```python
import functools

import jax
import jax.numpy as jnp
from jax import lax
from jax.experimental import pallas as pl
from jax.experimental.pallas import tpu as pltpu

F32 = jnp.float32
BF16 = jnp.bfloat16
I32 = jnp.int32

GRID_W = 64
N_HEADS = 16
Q_LORA = 1024
KV_LORA = 512
QK_NOPE = 128
QK_ROPE = 64
V_DIM = 128
ROPE_BASE = 10000.0
POOL_WINDOWS = (2, 4, 8, 16)
POOL_W = 1024
GM_CHUNK = 128
GM_GROUPS = 4
GM_W = 1024
N_EXPERTS = 16
CAPACITY_FACTOR = 2
EPS = 1e-6
LOG2_E = 1.4426950408889634

V7X_LANES = 128
V7X_VMEM_BYTES = 64 * 2**20
MIB = 2**20

ROW_TILE = 512
PROJ_COLS = 1024
NORM_CHUNK = 16
POOL_HALO = 64
ATTN_ROWS = 16
ATTN_KEYS = 512
ATTN_QUERIES = 512
MOE_TILE = 256
DMA_WORDS = 128


def _params(semantics, vmem_mib):
    return pltpu.CompilerParams(dimension_semantics=semantics, vmem_limit_bytes=int(vmem_mib * MIB))


def _pick(n, pref):
    best = None
    for t in range(V7X_LANES, min(n, pref) + 1, V7X_LANES):
        if n % t == 0:
            best = t
    assert best is not None, (n, pref)
    return best


def _ada_kernel(c_ref, w_ref, b_ref, o_ref):
    c = c_ref[...]
    s = c * jax.nn.sigmoid(c)
    acc = jnp.dot(s, w_ref[...], preferred_element_type=F32, precision=lax.Precision.HIGHEST)
    o_ref[...] = acc + b_ref[...]


def _ada(cc, w_ada, b_ada):
    L, D, N = w_ada.shape
    R = cc.shape[0]
    tn = _pick(N, 512)
    return pl.pallas_call(
        _ada_kernel,
        out_shape=jax.ShapeDtypeStruct((L, R, N), F32),
        grid=(L, N // tn),
        in_specs=[
            pl.BlockSpec((R, D), lambda l, j: (0, 0)),
            pl.BlockSpec((None, D, tn), lambda l, j: (l, 0, j)),
            pl.BlockSpec((None, 1, tn), lambda l, j: (l, 0, j)),
        ],
        out_specs=pl.BlockSpec((None, R, tn), lambda l, j: (l, 0, j)),
        compiler_params=_params(("arbitrary", "arbitrary"), 2 * 2 * D * tn * 4 / MIB + 8),
        name="ada",
    )(cc, w_ada, b_ada.reshape(L, 1, N))


def _norm_rows(x_ref, a, b, dst_refs, rows):
    def body(c, carry):
        r = pl.multiple_of(c * NORM_CHUNK, NORM_CHUNK)
        x = x_ref[pl.ds(r, NORM_CHUNK), :].astype(F32)
        ms = jnp.mean(x * x, axis=-1, keepdims=True)
        h = x * lax.rsqrt(ms + EPS) * a + b
        for d in dst_refs:
            d[pl.ds(r, NORM_CHUNK), :] = h.astype(d.dtype)
        return carry

    lax.fori_loop(0, rows // NORM_CHUNK, body, 0, unroll=2)


def _affine(a_ref, b_ref, mode):
    a = a_ref[...]
    if mode == "mod":
        a = 1.0 + a
    return a, b_ref[...]


def _in_proj_kernel(x_ref, sc_ref, sh_ref, w_ref, pa_ref, g_ref, h_ref, *, n_plain):
    j = pl.program_id(1)

    @pl.when(j == 0)
    def _():
        a, b = _affine(sc_ref, sh_ref, "mod")
        _norm_rows(x_ref, a, b, (h_ref,), x_ref.shape[0])

    acc = jnp.dot(h_ref[...], w_ref[...], preferred_element_type=F32)

    @pl.when(j < n_plain)
    def _():
        pa_ref[...] = acc

    @pl.when(j >= n_plain)
    def _():
        g_ref[...] = jax.nn.sigmoid(acc).astype(g_ref.dtype)


def _in_proj(x, sc, sh, w, *, n_plain, tm, tn, mod_fn):
    T, K = x.shape
    nj = w.shape[1] // tn
    vmem = (2 * tm * K * 4 + tm * K * 2 + 2 * K * tn * 2 + 2 * tm * tn * 6 + 2 * tm * tn * 4) / MIB + 6
    return pl.pallas_call(
        functools.partial(_in_proj_kernel, n_plain=n_plain),
        out_shape=(jax.ShapeDtypeStruct((T, n_plain * tn), F32),
                   jax.ShapeDtypeStruct((T, (nj - n_plain) * tn), BF16)),
        grid=(T // tm, nj),
        in_specs=[
            pl.BlockSpec((tm, K), lambda i, j: (i, 0)),
            pl.BlockSpec((None, 1, K), lambda i, j: (mod_fn(i), 0, 0)),
            pl.BlockSpec((None, 1, K), lambda i, j: (mod_fn(i), 0, 0)),
            pl.BlockSpec((K, tn), lambda i, j: (0, j)),
        ],
        out_specs=(pl.BlockSpec((tm, tn), lambda i, j: (i, jnp.minimum(j, n_plain - 1))),
                   pl.BlockSpec((tm, tn), lambda i, j: (i, jnp.maximum(j - n_plain, 0)))),
        scratch_shapes=[pltpu.VMEM((tm, K), BF16)],
        compiler_params=_params(("parallel", "arbitrary"), vmem),
        name="in_proj",
    )(x, sc, sh, w)


def _rope128(t, cos, sin):
    return t * cos + pltpu.roll(t, QK_ROPE, 1) * sin


def _q_kernel(x_ref, a_ref, b_ref, w_ref, cos_ref, sin_ref, o_ref, h_ref, *, scale):
    @pl.when(pl.program_id(1) == 0)
    def _():
        _norm_rows(x_ref, a_ref[...], b_ref[...], (h_ref,), x_ref.shape[0])

    acc = jnp.dot(h_ref[...], w_ref[...], preferred_element_type=F32)
    rot = _rope128(acc[:, QK_NOPE:], cos_ref[...], sin_ref[...])
    o_ref[:, :QK_NOPE] = (acc[:, :QK_NOPE] * scale).astype(o_ref.dtype)
    o_ref[:, QK_NOPE:] = (rot * scale).astype(o_ref.dtype)


def _kv_kernel(x_ref, a_ref, b_ref, w_ref, kr_ref, cos_ref, sin_ref, k_ref, v_ref, h_ref):
    @pl.when(pl.program_id(1) == 0)
    def _():
        _norm_rows(x_ref, a_ref[...], b_ref[...], (h_ref,), x_ref.shape[0])

    acc = jnp.dot(h_ref[...], w_ref[...], preferred_element_type=F32)
    k_ref[:, :QK_NOPE] = acc[:, :QK_NOPE].astype(k_ref.dtype)
    k_ref[:, QK_NOPE:] = _rope128(kr_ref[...], cos_ref[...], sin_ref[...]).astype(k_ref.dtype)
    v_ref[:, :V_DIM] = acc[:, QK_NOPE:].astype(v_ref.dtype)
    lane = lax.broadcasted_iota(I32, (v_ref.shape[0], V_DIM), 1)
    v_ref[:, V_DIM:] = jnp.where(lane == 0, 1.0, 0.0).astype(v_ref.dtype)


def _attn_kernel(*refs, tk, n_lat):
    if n_lat:
        q_ref, kc_ref, vc_ref, kl_ref, vl_ref, o_ref, s_ref, p_ref, acc_ref, m_ref, al_ref = refs
    else:
        q_ref, kc_ref, vc_ref, o_ref, s_ref, p_ref, acc_ref, m_ref, al_ref = refs
    tq = q_ref.shape[0]
    ctx = kc_ref.shape[0]

    def scores(k):
        return lax.dot_general(q_ref[...], k, (((1,), (1,)), ((), ())), preferred_element_type=F32)

    def softmax(slot, width):
        for g in range(tq // ATTN_ROWS):
            rows = slice(g * ATTN_ROWS, (g + 1) * ATTN_ROWS)
            sg = s_ref[slot, rows, :width]
            m_old = m_ref[rows, :]
            m_new = jnp.maximum(m_old, jnp.max(sg, axis=-1, keepdims=True))
            p_ref[slot, rows, :width] = jnp.exp2(sg - m_new).astype(BF16)
            al_ref[slot, rows, :] = jnp.exp2(m_old - m_new)
            m_ref[rows, :] = m_new

    def accumulate(slot, width, v):
        acc_ref[...] = al_ref[slot] * acc_ref[...] + jnp.dot(p_ref[slot, :, :width], v,
                                                             preferred_element_type=F32)

    def lat(ref, c):
        if isinstance(c, int):
            return ref[c * tk:(c + 1) * tk, :]
        return ref[pl.ds(pl.multiple_of(c * tk, tk), tk), :]

    m_ref[...] = jnp.full(m_ref.shape, -jnp.inf, F32)
    acc_ref[...] = jnp.zeros(acc_ref.shape, F32)
    s_ref[0, :, :ctx] = scores(kc_ref[...])
    if n_lat:
        s_ref[1] = scores(kl_ref[0:tk, :])
    softmax(0, ctx)
    if n_lat:
        def step(c, odd, last):
            if not last:
                s_ref[1 if odd else 0] = scores(lat(kl_ref, jnp.minimum(c + 1, n_lat - 1)))
            softmax(0 if odd else 1, tk)
            accumulate(1 if odd else 0, tk, lat(vl_ref, c - 1))

        if n_lat > 1:
            s_ref[0] = scores(kl_ref[tk:2 * tk, :])
        softmax(1, tk)
        accumulate(0, ctx, vc_ref[...])
        pairs = (n_lat - 1) // 2

        def body(j, carry):
            step(1 + 2 * j, True, False)
            step(2 + 2 * j, False, False)
            return carry

        lax.fori_loop(0, pairs, body, 0)
        if (n_lat - 1) % 2:
            step(n_lat - 1, True, True)
        accumulate(n_lat % 2, tk, vl_ref[(n_lat - 1) * tk:n_lat * tk, :])
    else:
        accumulate(0, ctx, vc_ref[...])
    acc = acc_ref[...]
    o_ref[...] = (acc[:, :V_DIM] / acc[:, V_DIM:V_DIM + 1]).astype(o_ref.dtype)


def _attn_scratch(tq, tk, dv):
    return [pltpu.VMEM((2, tq, tk), F32), pltpu.VMEM((2, tq, tk), BF16), pltpu.VMEM((tq, dv), F32),
            pltpu.VMEM((tq, 1), F32), pltpu.VMEM((2, tq, 1), F32)]


def _attention(qc, kc, vc, *, B, SEQ, CTX):
    H, T, dq = qc.shape
    dv = vc.shape[2]
    NL = B * SEQ
    tq = min(ATTN_QUERIES, SEQ)
    tk = max(min(ATTN_KEYS, SEQ), CTX)
    assert SEQ % tk == 0 and SEQ % tq == 0 and tk >= CTX and tq % ATTN_ROWS == 0 and CTX % ATTN_ROWS == 0
    nq = SEQ // tq
    cblk = NL // CTX
    lat = pl.pallas_call(
        functools.partial(_attn_kernel, tk=tk, n_lat=SEQ // tk),
        out_shape=jax.ShapeDtypeStruct((NL, H * V_DIM), BF16),
        grid=(B, H, nq),
        in_specs=[
            pl.BlockSpec((None, tq, dq), lambda b, h, i: (h, b * nq + i, 0)),
            pl.BlockSpec((None, CTX, dq), lambda b, h, i: (h, cblk + b, 0)),
            pl.BlockSpec((None, CTX, dv), lambda b, h, i: (h, cblk + b, 0)),
            pl.BlockSpec((None, SEQ, dq), lambda b, h, i: (h, b, 0)),
            pl.BlockSpec((None, SEQ, dv), lambda b, h, i: (h, b, 0)),
        ],
        out_specs=pl.BlockSpec((tq, V_DIM), lambda b, h, i: (b * nq + i, h)),
        scratch_shapes=_attn_scratch(tq, tk, dv),
        compiler_params=_params(("parallel", "parallel", "arbitrary"),
                                2 * SEQ * (dq + dv) * 2 / MIB + 24),
        name="attn_latent",
    )(qc, kc, vc, kc, vc)
    ctx = pl.pallas_call(
        functools.partial(_attn_kernel, tk=tk, n_lat=0),
        out_shape=jax.ShapeDtypeStruct((B * CTX, H * V_DIM), BF16),
        grid=(B, H),
        in_specs=[
            pl.BlockSpec((None, CTX, dq), lambda b, h: (h, cblk + b, 0)),
            pl.BlockSpec((None, CTX, dq), lambda b, h: (h, cblk + b, 0)),
            pl.BlockSpec((None, CTX, dv), lambda b, h: (h, cblk + b, 0)),
        ],
        out_specs=pl.BlockSpec((CTX, V_DIM), lambda b, h: (b, h)),
        scratch_shapes=_attn_scratch(CTX, CTX, dv),
        compiler_params=_params(("parallel", "arbitrary"), 16),
        name="attn_ctx",
    )(qc, kc, vc)
    return jnp.concatenate([lat, ctx], axis=0)


def _pool_kernel(prev_ref, x_ref, next_ref, w_ref, ps_ref, o_ref, *, NL, SEQ, CTX):
    tm, width = x_ref.shape
    gw = width // len(POOL_WINDOWS)
    row0 = pl.program_id(0) * tm
    xm = x_ref[...]
    xcat = jnp.concatenate([prev_ref[...], xm, next_ref[...]], axis=0)
    hi_part = xcat.astype(BF16)
    lo_part = (xcat - hi_part.astype(F32)).astype(BF16)

    r = row0 + lax.broadcasted_iota(I32, (tm, 1), 0)
    s = row0 - POOL_HALO + lax.broadcasted_iota(I32, (1, tm + 2 * POOL_HALO), 1)
    lat_lo = (row0 // SEQ) * SEQ
    ctx_lo = jnp.full((tm, 1), row0, I32)
    for k in range(1, max(tm // CTX, 1)):
        ctx_lo = jnp.where(r >= row0 + k * CTX, row0 + k * CTX, ctx_lo)
    is_lat = row0 < NL
    seq_lo = jnp.where(is_lat, lat_lo, ctx_lo)
    seq_hi = seq_lo + jnp.where(is_lat, SEQ, CTX)

    for g, win in enumerate(POOL_WINDOWS):
        left = win // 2
        right = win - 1 - left
        lo = jnp.maximum(r - left, seq_lo)
        hi = jnp.minimum(r + right + 1, seq_hi)
        band = jnp.where((s >= lo) & (s < hi), 1.0, 0.0).astype(BF16)
        cols = slice(g * gw, (g + 1) * gw)
        tot = (jnp.dot(band, hi_part[:, cols], preferred_element_type=F32)
               + jnp.dot(band, lo_part[:, cols], preferred_element_type=F32))
        d = tot / (hi - lo).astype(F32) - xm[:, cols]
        y = jnp.dot(d.astype(BF16), w_ref[g], preferred_element_type=F32)
        o_ref[:, cols] = (y * ps_ref[:, cols]).astype(o_ref.dtype)


def _pool(pa, col, w_pool, pool_scale, *, tm, NL, SEQ, CTX):
    T = pa.shape[0]
    hb = tm // POOL_HALO
    last = T // POOL_HALO - 1
    return pl.pallas_call(
        functools.partial(_pool_kernel, NL=NL, SEQ=SEQ, CTX=CTX),
        out_shape=jax.ShapeDtypeStruct((T, POOL_W), BF16),
        grid=(T // tm,),
        in_specs=[
            pl.BlockSpec((POOL_HALO, POOL_W), lambda i: (jnp.maximum(i * hb - 1, 0), col)),
            pl.BlockSpec((tm, POOL_W), lambda i: (i, col)),
            pl.BlockSpec((POOL_HALO, POOL_W), lambda i: (jnp.minimum((i + 1) * hb, last), col)),
            pl.BlockSpec(w_pool.shape, lambda i: (0, 0, 0)),
            pl.BlockSpec((1, POOL_W), lambda i: (0, 0)),
        ],
        out_specs=pl.BlockSpec((tm, POOL_W), lambda i: (i, 0)),
        compiler_params=_params(("parallel",), 40),
        name="pool",
    )(pa, pa, pa, w_pool, pool_scale)


def _gelu(z):
    return 0.5 * z * (1.0 + lax.erf(z * (2.0 ** -0.5)))


def _gm_kernel(u_ref, v_ref, ws_ref, bias_ref, o_ref):
    tm = u_ref.shape[0]
    gw = GM_W // GM_GROUPS

    def body(c, carry):
        r = pl.multiple_of(c * GM_CHUNK, GM_CHUNK)
        zu = _gelu(u_ref[pl.ds(r, GM_CHUNK), :])
        zv = _gelu(v_ref[pl.ds(r, GM_CHUNK), :])
        mu = jnp.mean(zv, axis=-1, keepdims=True)
        var = jnp.mean(jnp.square(zv - mu), axis=-1, keepdims=True)
        vn = ((zv - mu) * lax.rsqrt(var + EPS)).astype(BF16)
        for g in range(GM_GROUPS):
            cols = slice(g * gw, (g + 1) * gw)
            sv = jnp.dot(ws_ref[g], vn[:, cols], preferred_element_type=F32) + bias_ref[:, cols]
            o_ref[pl.ds(r, GM_CHUNK), cols] = (zu[:, cols] * sv).astype(o_ref.dtype)
        return carry

    lax.fori_loop(0, tm // GM_CHUNK, body, 0)


def _gm(pa, ucol, w_s, bias_full, *, tm):
    T = pa.shape[0]
    return pl.pallas_call(
        _gm_kernel,
        out_shape=jax.ShapeDtypeStruct((T, GM_W), BF16),
        grid=(T // tm,),
        in_specs=[
            pl.BlockSpec((tm, GM_W), lambda i: (i, ucol)),
            pl.BlockSpec((tm, GM_W), lambda i: (i, ucol + 1)),
            pl.BlockSpec(w_s.shape, lambda i: (0, 0, 0)),
            pl.BlockSpec(bias_full.shape, lambda i: (0, 0)),
        ],
        out_specs=pl.BlockSpec((tm, GM_W), lambda i: (i, 0)),
        compiler_params=_params(("parallel",), 32),
        name="gmlp",
    )(pa, pa, w_s, bias_full)


def _branch_kernel(at_ref, po_ref, gm_ref, w_ref, g0_ref, g1_ref, g2_ref, o_ref):
    ka = at_ref.shape[1]
    kp = po_ref.shape[1]
    y = g0_ref[...].astype(F32) * jnp.dot(at_ref[...], w_ref[0:ka, :], preferred_element_type=F32)
    y += g1_ref[...].astype(F32) * jnp.dot(po_ref[...], w_ref[ka:ka + kp, :], preferred_element_type=F32)
    y += g2_ref[...].astype(F32) * jnp.dot(gm_ref[...], w_ref[ka + kp:, :], preferred_element_type=F32)
    o_ref[...] = y.astype(o_ref.dtype)


def _branch(attn, pool, gm, w_branch, gates, *, tm, tn):
    T = attn.shape[0]
    K, D = w_branch.shape
    nj = D // tn
    return pl.pallas_call(
        _branch_kernel,
        out_shape=jax.ShapeDtypeStruct((T, D), BF16),
        grid=(T // tm, nj),
        in_specs=[
            pl.BlockSpec((tm, attn.shape[1]), lambda i, j: (i, 0)),
            pl.BlockSpec((tm, pool.shape[1]), lambda i, j: (i, 0)),
            pl.BlockSpec((tm, gm.shape[1]), lambda i, j: (i, 0)),
            pl.BlockSpec((K, tn), lambda i, j: (0, j)),
            pl.BlockSpec((tm, tn), lambda i, j: (i, j)),
            pl.BlockSpec((tm, tn), lambda i, j: (i, nj + j)),
            pl.BlockSpec((tm, tn), lambda i, j: (i, 2 * nj + j)),
        ],
        out_specs=pl.BlockSpec((tm, tn), lambda i, j: (i, j)),
        compiler_params=_params(("parallel", "arbitrary"),
                                (2 * tm * K * 2 + 2 * K * tn * 2 + 8 * tm * tn * 2 + 4 * tm * tn * 4) / MIB + 8),
        name="branch_mix",
    )(attn, pool, gm, w_branch, gates, gates, gates)


def _mm_res_kernel(a_ref, w_ref, r_ref, g_ref, o_ref):
    acc = jnp.dot(a_ref[...], w_ref[...], preferred_element_type=F32)
    o_ref[...] = r_ref[...] + g_ref[...] * acc


def _mm_res(a, w, res, gate, *, tm, tn, mod_fn):
    T, K = a.shape
    N = w.shape[1]
    return pl.pallas_call(
        _mm_res_kernel,
        out_shape=jax.ShapeDtypeStruct((T, N), F32),
        grid=(T // tm, N // tn),
        in_specs=[
            pl.BlockSpec((tm, K), lambda i, j: (i, 0)),
            pl.BlockSpec((K, tn), lambda i, j: (0, j)),
            pl.BlockSpec((tm, tn), lambda i, j: (i, j)),
            pl.BlockSpec((None, 1, tn), lambda i, j: (mod_fn(i), 0, j)),
        ],
        out_specs=pl.BlockSpec((tm, tn), lambda i, j: (i, j)),
        input_output_aliases={2: 0},
        compiler_params=_params(("parallel", "arbitrary"),
                                (2 * tm * K * 2 + 2 * K * tn * 2 + 6 * tm * tn * 4) / MIB + 8),
        name="out_proj",
    )(a, w, res, gate)


def _mod_router_kernel(x_ref, sc_ref, sh_ref, wr_ref, h_ref, lg_ref, hb_ref):
    a, b = _affine(sc_ref, sh_ref, "mod")
    _norm_rows(x_ref, a, b, (h_ref, hb_ref), x_ref.shape[0])
    lg_ref[...] = lax.dot_general(wr_ref[...], hb_ref[...], (((1,), (1,)), ((), ())),
                                  preferred_element_type=F32)


def _mod_router(xs, sc, sh, wr_t, *, tm, mod_fn):
    T, D = xs.shape
    E = wr_t.shape[0]
    return pl.pallas_call(
        _mod_router_kernel,
        out_shape=(jax.ShapeDtypeStruct((T, D), F32), jax.ShapeDtypeStruct((E, T), F32)),
        grid=(T // tm,),
        in_specs=[
            pl.BlockSpec((tm, D), lambda i: (i, 0)),
            pl.BlockSpec((None, 1, D), lambda i: (mod_fn(i), 0, 0)),
            pl.BlockSpec((None, 1, D), lambda i: (mod_fn(i), 0, 0)),
            pl.BlockSpec((E, D), lambda i: (0, 0)),
        ],
        out_specs=(pl.BlockSpec((tm, D), lambda i: (i, 0)), pl.BlockSpec((E, tm), lambda i: (0, i))),
        scratch_shapes=[pltpu.VMEM((tm, D), BF16)],
        compiler_params=_params(("parallel",), (4 * tm * D * 4 + tm * D * 2) / MIB + 8),
        name="mod_router",
    )(xs, sc, sh, wr_t)


def _route_kernel(lg_ref, code_ref, *, cap):
    lg = lg_ref[...]
    E, n = lg.shape
    mx = jnp.max(lg, axis=0, keepdims=True)
    ex = jnp.exp(lg - mx)
    aff = ex / jnp.sum(ex, axis=0, keepdims=True)
    bits = pltpu.bitcast(aff, I32)

    def count(mask):
        return jnp.sum(jnp.where(mask, 1.0, 0.0), axis=1, keepdims=True)

    v = jnp.zeros((E, 1), I32)
    for bit in range(30, -1, -1):
        cand = v | (1 << bit)
        v = jnp.where(count(bits >= cand) >= cap, cand, v)
    gt = bits > v
    eq = bits == v
    need = cap - count(gt)
    t = lax.broadcasted_iota(I32, (E, n), 1)
    u = jnp.zeros((E, 1), I32)
    for bit in range(max(n - 1, 1).bit_length() - 1, -1, -1):
        cand = u | (1 << bit)
        u = jnp.where(count(eq & (t < cand)) < need, cand, u)
    sel = gt | (eq & (t <= u))
    code_ref[...] = jnp.where(sel, bits | jnp.int32(-2**31), bits)


def _route(logits_t, *, n, cap, blk0, sets):
    E = logits_t.shape[0]
    return pl.pallas_call(
        functools.partial(_route_kernel, cap=cap),
        out_shape=jax.ShapeDtypeStruct((sets, E, n), I32),
        grid=(sets,),
        in_specs=[pl.BlockSpec((E, n), lambda s: (0, blk0 + s))],
        out_specs=pl.BlockSpec((None, E, n), lambda s: (s, 0, 0)),
        compiler_params=_params(("parallel",), 32),
        name="route_select",
    )(logits_t)


def _compact_kernel(code_hbm, idx_hbm, gb_hbm, code_s, idx_s, gb_s, sem, *, n, cap):
    s = pl.program_id(0)
    e = pl.program_id(1)
    cin = pltpu.make_async_copy(code_hbm.at[s, e], code_s, sem)
    cin.start()
    cin.wait()

    def body(t, c):
        w = code_s[t]
        idx_s[c] = t
        gb_s[c] = w & 0x7FFFFFFF
        return c + lax.shift_right_logical(w, jnp.int32(31))

    lax.fori_loop(0, n, body, jnp.int32(0), unroll=8)
    cap_pad = idx_hbm.shape[-1]

    def clear(p, carry):
        idx_s[p] = 0
        gb_s[p] = 0
        return carry

    lax.fori_loop(cap, cap_pad, clear, 0)
    o1 = pltpu.make_async_copy(idx_s.at[pl.ds(0, cap_pad)], idx_hbm.at[s, e], sem)
    o1.start()
    o1.wait()
    o2 = pltpu.make_async_copy(gb_s.at[pl.ds(0, cap_pad)], gb_hbm.at[s, e], sem)
    o2.start()
    o2.wait()


def _compact(code, *, cap):
    sets, E, n = code.shape
    cap_pad = -(-cap // DMA_WORDS) * DMA_WORDS
    idx, gbits = pl.pallas_call(
        functools.partial(_compact_kernel, n=n, cap=cap),
        out_shape=(jax.ShapeDtypeStruct((sets, E, cap_pad), I32), jax.ShapeDtypeStruct((sets, E, cap_pad), I32)),
        grid=(sets, E),
        in_specs=[pl.BlockSpec(memory_space=pl.ANY)],
        out_specs=(pl.BlockSpec(memory_space=pl.ANY), pl.BlockSpec(memory_space=pl.ANY)),
        scratch_shapes=[pltpu.SMEM((n,), I32), pltpu.SMEM((cap_pad + DMA_WORDS,), I32),
                        pltpu.SMEM((cap_pad + DMA_WORDS,), I32), pltpu.SemaphoreType.DMA(())],
        compiler_params=_params(("arbitrary", "arbitrary"), 16),
        name="route_compact",
    )(code)
    return idx[:, :, :cap], gbits[:, :, :cap]


def _row_copy(hbm, vmem, slot, r, row, sem, to_hbm):
    h, v = hbm.at[pl.ds(row, 1)], vmem.at[slot, pl.ds(r, 1)]
    return pltpu.make_async_copy(v, h, sem.at[slot]) if to_hbm else pltpu.make_async_copy(h, v, sem.at[slot])


def _row_starts(hbm, vmem, idx_s, slot, sem, rows, *, to_hbm=False):
    for r in range(rows):
        _row_copy(hbm, vmem, slot, r, idx_s[slot, r], sem, to_hbm).start()


def _row_waits(hbm, vmem, slot, sem, rows, *, to_hbm=False):
    def body(r, carry):
        _row_copy(hbm, vmem, slot, r, 0, sem, to_hbm).wait()
        return carry

    lax.fori_loop(0, rows, body, 0, unroll=8)


def _idx_copy(idx_hbm, idx_s, slot, sem, step):
    return pltpu.make_async_copy(idx_hbm.at[step], idx_s.at[slot], sem)


def _step_ids():
    step = pl.program_id(0) * pl.num_programs(1) + pl.program_id(1)
    return step, pl.num_programs(0) * pl.num_programs(1), lax.rem(step, 2)


def _ffn1_kernel(idx_hbm, h_hbm, wg_ref, wu_ref, o_ref, xbuf, idx_s, sem_i, sem_g):
    tm = xbuf.shape[1]
    step, total, slot = _step_ids()

    def fetch(st, sl):
        cp = _idx_copy(idx_hbm, idx_s, sl, sem_i, st)
        cp.start()
        cp.wait()
        _row_starts(h_hbm, xbuf, idx_s, sl, sem_g, tm)

    @pl.when(step == 0)
    def _():
        fetch(0, 0)

    @pl.when(step + 1 < total)
    def _():
        fetch(step + 1, 1 - slot)

    _row_waits(h_hbm, xbuf, slot, sem_g, tm)
    xb = xbuf[slot].astype(BF16)
    a = jnp.dot(xb, wg_ref[...], preferred_element_type=F32)
    u = jnp.dot(xb, wu_ref[...], preferred_element_type=F32)
    o_ref[...] = (a * jax.nn.sigmoid(a) * u).astype(o_ref.dtype)


def _ffn1(idx_tiles, h2, w_gate, w_up, *, tm):
    E, D, F = w_gate.shape
    tiles = idx_tiles.shape[0] // E
    return pl.pallas_call(
        _ffn1_kernel,
        out_shape=jax.ShapeDtypeStruct((E, tiles * tm, F), BF16),
        grid=(E, tiles),
        in_specs=[
            pl.BlockSpec(memory_space=pl.ANY),
            pl.BlockSpec(memory_space=pl.ANY),
            pl.BlockSpec((None, D, F), lambda e, m: (e, 0, 0)),
            pl.BlockSpec((None, D, F), lambda e, m: (e, 0, 0)),
        ],
        out_specs=pl.BlockSpec((None, tm, F), lambda e, m: (e, m, 0)),
        scratch_shapes=[pltpu.VMEM((2, tm, D), F32), pltpu.SMEM((2, max(tm, DMA_WORDS)), I32),
                        pltpu.SemaphoreType.DMA(()), pltpu.SemaphoreType.DMA((2,))],
        compiler_params=_params(("arbitrary", "arbitrary"),
                                (4 * D * F * 2 + tm * D * 10 + 4 * tm * F * 4) / MIB + 6),
        name="expert_ffn_in",
    )(idx_tiles, h2, w_gate, w_up)


def _ffn2_kernel(idx_hbm, hh_ref, wd_ref, gate_ref, g2_ref, acc_in, o_hbm, abuf, idx_s, sem_i, sem_g, sem_s,
                 *, overlap):
    del acc_in
    tm = abuf.shape[1]
    step, total, slot = _step_ids()

    def fetch(st, sl):
        cp = _idx_copy(idx_hbm, idx_s, sl, sem_i, st)
        cp.start()
        cp.wait()
        _row_starts(o_hbm, abuf, idx_s, sl, sem_g, tm)

    if overlap:
        @pl.when(step == 0)
        def _():
            fetch(0, 0)
    else:
        fetch(step, slot)

    y = jnp.dot(hh_ref[...], wd_ref[...], preferred_element_type=F32)
    y = y * gate_ref[...] * g2_ref[...]
    _row_waits(o_hbm, abuf, slot, sem_g, tm)
    abuf[slot] = abuf[slot] + y
    _row_starts(o_hbm, abuf, idx_s, slot, sem_s, tm, to_hbm=True)

    if overlap:
        @pl.when(step > 0)
        def _():
            _row_waits(o_hbm, abuf, 1 - slot, sem_s, tm, to_hbm=True)

        @pl.when(step + 1 < total)
        def _():
            fetch(step + 1, 1 - slot)

        @pl.when(step + 1 == total)
        def _():
            _row_waits(o_hbm, abuf, slot, sem_s, tm, to_hbm=True)
    else:
        _row_waits(o_hbm, abuf, slot, sem_s, tm, to_hbm=True)


def _ffn2_scatter(idx_tiles, hh, w_down, gates, g2, xs, *, tm, mod_fn, overlap):
    E, F, D = w_down.shape
    tiles = idx_tiles.shape[0] // E
    return pl.pallas_call(
        functools.partial(_ffn2_kernel, overlap=overlap),
        out_shape=jax.ShapeDtypeStruct(xs.shape, xs.dtype),
        grid=(E, tiles),
        in_specs=[
            pl.BlockSpec(memory_space=pl.ANY),
            pl.BlockSpec((None, tm, F), lambda e, m: (e, m, 0)),
            pl.BlockSpec((None, F, D), lambda e, m: (e, 0, 0)),
            pl.BlockSpec((None, tm, 1), lambda e, m: (e, m, 0)),
            pl.BlockSpec((None, 1, D), lambda e, m: (mod_fn(m), 0, 0)),
            pl.BlockSpec(memory_space=pl.ANY),
        ],
        out_specs=pl.BlockSpec(memory_space=pl.ANY),
        scratch_shapes=[pltpu.VMEM((2, tm, D), F32), pltpu.SMEM((2, max(tm, DMA_WORDS)), I32),
                        pltpu.SemaphoreType.DMA(()), pltpu.SemaphoreType.DMA((2,)), pltpu.SemaphoreType.DMA((2,))],
        input_output_aliases={5: 0},
        compiler_params=_params(("arbitrary", "arbitrary"),
                                (2 * F * D * 2 + 4 * tm * D * 4 + 4 * tm * F * 2) / MIB + 8),
        name="expert_ffn_out",
    )(idx_tiles, hh, w_down, gates, g2, xs)


def _moe(xs, logits_t, h2, g2, w_gate, w_up, w_down, *, n, sets, row0, mod_row):
    E = logits_t.shape[0]
    cap = (CAPACITY_FACTOR * n) // E
    code = _route(logits_t, n=n, cap=cap, blk0=row0 // n, sets=sets)
    idx, gbits = _compact(code, cap=cap)
    tm = min(MOE_TILE, cap if mod_row is None else sets * cap)
    assert (sets * cap) % tm == 0 and (mod_row is not None or cap % tm == 0)
    rows = idx + (row0 + n * jnp.arange(sets, dtype=I32))[:, None, None]
    idx_tiles = jnp.transpose(rows, (1, 0, 2)).reshape(E * sets * cap // tm, tm)
    idx_tiles = jnp.pad(idx_tiles, ((0, 0), (0, max(DMA_WORDS - tm, 0))))
    gates = lax.bitcast_convert_type(jnp.transpose(gbits, (1, 0, 2)), F32).reshape(E, sets * cap, 1)
    hh = _ffn1(idx_tiles, h2, w_gate, w_up, tm=tm)
    mod_fn = (lambda m: m // (cap // tm)) if mod_row is None else (lambda m: mod_row)
    overlap = mod_row is None and sets >= 2
    return _ffn2_scatter(idx_tiles, hh, w_down, gates, g2, xs, tm=tm, mod_fn=mod_fn, overlap=overlap)


def _final_norm_kernel(x_ref, g_ref, z_ref, o_ref):
    _norm_rows(x_ref, g_ref[...], z_ref[...], (o_ref,), x_ref.shape[0])


def _final_norm(xs, gain, rows, *, tm):
    D = xs.shape[1]
    return pl.pallas_call(
        _final_norm_kernel,
        out_shape=jax.ShapeDtypeStruct((rows, D), F32),
        grid=(rows // tm,),
        in_specs=[
            pl.BlockSpec((tm, D), lambda i: (i, 0)),
            pl.BlockSpec((1, D), lambda i: (0, 0)),
            pl.BlockSpec((1, D), lambda i: (0, 0)),
        ],
        out_specs=pl.BlockSpec((tm, D), lambda i: (i, 0)),
        compiler_params=_params(("parallel",), 4 * tm * D * 4 / MIB + 8),
        name="final_norm",
    )(xs, gain.reshape(1, D), jnp.zeros((1, D), F32))


def _rope_tables(B, SEQ, CTX):
    rows = SEQ // GRID_W
    row = jnp.broadcast_to(jnp.arange(rows, dtype=F32)[:, None], (rows, GRID_W)).reshape(-1)
    col = jnp.broadcast_to(jnp.arange(GRID_W, dtype=F32)[None, :], (rows, GRID_W)).reshape(-1)
    n_freq = QK_ROPE // 4
    inv = ROPE_BASE ** (-jnp.arange(n_freq, dtype=F32) / n_freq)
    ar = row[:, None] * inv
    ac = col[:, None] * inv
    cos = jnp.concatenate([jnp.cos(ar), jnp.cos(ar), jnp.cos(ac), jnp.cos(ac)], axis=-1)
    sin = jnp.concatenate([-jnp.sin(ar), jnp.sin(ar), -jnp.sin(ac), jnp.sin(ac)], axis=-1)
    pad = jnp.zeros((SEQ, 2 * QK_ROPE - QK_ROPE), F32)
    cos_l = jnp.concatenate([cos, pad], axis=-1)
    sin_l = jnp.concatenate([sin, pad], axis=-1)
    cos_c = jnp.concatenate([jnp.ones((B * CTX, QK_ROPE), F32), jnp.zeros((B * CTX, QK_ROPE), F32)], axis=-1)
    cos_t = jnp.concatenate([jnp.tile(cos_l, (B, 1)), cos_c], axis=0)
    sin_t = jnp.concatenate([jnp.tile(sin_l, (B, 1)), jnp.zeros((B * CTX, 2 * QK_ROPE), F32)], axis=0)
    return cos_t, sin_t


def _swap_perm():
    q = QK_ROPE // 4
    return jnp.concatenate([jnp.arange(q, 2 * q), jnp.arange(0, q), jnp.arange(3 * q, 4 * q), jnp.arange(2 * q, 3 * q)])


def kernel(x, c, ctx, c_ctx, w_ada, b_ada, w_in, q_norm_g, w_uq, kv_norm_g, w_ukv, w_pool, pool_scale,
           w_spatial, b_spatial, w_branch, w_out, w_router, w_gate, w_up, w_down, final_g):
    B, SEQ, D = x.shape
    CTX = ctx.shape[1]
    L = w_ada.shape[0]
    H = N_HEADS
    NL, NC = B * SEQ, B * CTX
    T = NL + NC
    tm = min(ROW_TILE, NC)
    assert SEQ % tm == 0 and NC % tm == 0 and tm % CTX == 0 and CTX % GM_CHUNK == 0 and SEQ % GM_CHUNK == 0
    assert QK_NOPE == V7X_LANES and 2 * QK_ROPE == V7X_LANES and V_DIM == V7X_LANES
    assert Q_LORA == POOL_W == GM_W and Q_LORA % KV_LORA == 0 and KV_LORA % V7X_LANES == 0
    tiles_per_seq = SEQ // tm

    def mod_tile(i):
        return jnp.minimum(i // tiles_per_seq, B)

    kv_start = Q_LORA
    kr_start = kv_start + KV_LORA
    pool_start = kr_start + QK_ROPE
    gm_start = pool_start + POOL_W
    gate_start = gm_start + 2 * GM_W
    wa_cols = Q_LORA + POOL_W + 2 * GM_W + KV_LORA + 2 * QK_ROPE
    tn_in = _pick(3 * D, PROJ_COLS)
    wa_pad = -wa_cols % tn_in
    perm = _swap_perm()
    dh = QK_NOPE + QK_ROPE
    scale = float(dh) ** -0.5 * LOG2_E

    cos_t, sin_t = _rope_tables(B, SEQ, CTX)
    xs = jnp.concatenate([x.reshape(NL, D), ctx.reshape(NC, D)], axis=0)

    cc = jnp.zeros((8, D), F32).at[:B].set(c).at[B].set(c_ctx)
    mods = _ada(cc, w_ada, b_ada)[:, :B + 1].reshape(L, B + 1, 6, 1, D)

    zeros_q = jnp.zeros((1, 1, Q_LORA), F32)
    zeros_kv = jnp.zeros((1, 1, KV_LORA), F32)

    for l in range(L):
        sh1, sc1, g1, sh2, sc2, g2 = (mods[l, :, k] for k in range(6))
        wl = w_in[l]
        kr_w = wl[:, kr_start:kr_start + QK_ROPE]
        w_all = jnp.concatenate(
            [wl[:, :Q_LORA], wl[:, pool_start:pool_start + POOL_W], wl[:, gm_start:gm_start + 2 * GM_W],
             wl[:, kv_start:kv_start + KV_LORA], kr_w, kr_w[:, perm], jnp.zeros((D, wa_pad), F32),
             wl[:, gate_start:]], axis=1).astype(BF16)
        uq = w_uq[l].reshape(Q_LORA, H, dh)
        uq_ext = jnp.concatenate([uq, uq[:, :, QK_NOPE:][:, :, perm]], axis=-1).reshape(Q_LORA, H * (dh + QK_ROPE))
        uq_ext = uq_ext.astype(BF16)
        ukv = w_ukv[l].astype(BF16)

        pa, gates = _in_proj(xs, sc1, sh1, w_all, n_plain=(wa_cols + wa_pad) // tn_in, tm=tm, tn=tn_in,
                             mod_fn=mod_tile)
        col_pool = Q_LORA // POOL_W
        col_gm = (Q_LORA + POOL_W) // GM_W
        col_kv = (Q_LORA + POOL_W + 2 * GM_W) // KV_LORA
        col_kr = (Q_LORA + POOL_W + 2 * GM_W + KV_LORA) // V7X_LANES

        hq = dh + QK_ROPE
        qc = pl.pallas_call(
            functools.partial(_q_kernel, scale=scale),
            out_shape=jax.ShapeDtypeStruct((H, T, hq), BF16),
            grid=(T // tm, H),
            in_specs=[
                pl.BlockSpec((tm, Q_LORA), lambda i, h: (i, 0)),
                pl.BlockSpec((None, 1, Q_LORA), lambda i, h: (0, 0, 0)),
                pl.BlockSpec((None, 1, Q_LORA), lambda i, h: (0, 0, 0)),
                pl.BlockSpec((Q_LORA, hq), lambda i, h: (0, h)),
                pl.BlockSpec((tm, 2 * QK_ROPE), lambda i, h: (i, 0)),
                pl.BlockSpec((tm, 2 * QK_ROPE), lambda i, h: (i, 0)),
            ],
            out_specs=pl.BlockSpec((None, tm, hq), lambda i, h: (h, i, 0)),
            scratch_shapes=[pltpu.VMEM((tm, Q_LORA), BF16)],
            compiler_params=_params(("parallel", "arbitrary"), 24),
            name="q_proj",
        )(pa, q_norm_g[l].reshape(1, 1, Q_LORA), zeros_q, uq_ext, cos_t, sin_t)
        kc, vc = pl.pallas_call(
            _kv_kernel,
            out_shape=(jax.ShapeDtypeStruct((H, T, hq), BF16), jax.ShapeDtypeStruct((H, T, 2 * V_DIM), BF16)),
            grid=(T // tm, H),
            in_specs=[
                pl.BlockSpec((tm, KV_LORA), lambda i, h: (i, col_kv)),
                pl.BlockSpec((None, 1, KV_LORA), lambda i, h: (0, 0, 0)),
                pl.BlockSpec((None, 1, KV_LORA), lambda i, h: (0, 0, 0)),
                pl.BlockSpec((KV_LORA, QK_NOPE + V_DIM), lambda i, h: (0, h)),
                pl.BlockSpec((tm, 2 * QK_ROPE), lambda i, h: (i, col_kr)),
                pl.BlockSpec((tm, 2 * QK_ROPE), lambda i, h: (i, 0)),
                pl.BlockSpec((tm, 2 * QK_ROPE), lambda i, h: (i, 0)),
            ],
            out_specs=(pl.BlockSpec((None, tm, hq), lambda i, h: (h, i, 0)),
                       pl.BlockSpec((None, tm, 2 * V_DIM), lambda i, h: (h, i, 0))),
            scratch_shapes=[pltpu.VMEM((tm, KV_LORA), BF16)],
            compiler_params=_params(("parallel", "arbitrary"), 24),
            name="kv_proj",
        )(pa, kv_norm_g[l].reshape(1, 1, KV_LORA), zeros_kv, ukv, pa, cos_t, sin_t)
        attn = _attention(qc, kc, vc, B=B, SEQ=SEQ, CTX=CTX)

        pool = _pool(pa, col_pool, w_pool[l].astype(BF16), pool_scale[l].reshape(1, POOL_W),
                     tm=tm, NL=NL, SEQ=SEQ, CTX=CTX)
        bias_full = jnp.repeat(jnp.transpose(b_spatial[l]), GM_W // GM_GROUPS, axis=1)
        gm = _gm(pa, col_gm, w_spatial[l].astype(BF16), bias_full, tm=tm)

        y = _branch(attn, pool, gm, w_branch[l].astype(BF16), gates, tm=tm, tn=_pick(D, PROJ_COLS))
        xs = _mm_res(y, w_out[l].astype(BF16), xs, g1, tm=tm, tn=_pick(D, PROJ_COLS), mod_fn=mod_tile)

        h2, logits_t = _mod_router(xs, sc2, sh2, jnp.transpose(w_router[l]).astype(BF16), tm=tm, mod_fn=mod_tile)
        wg, wu, wd = w_gate[l].astype(BF16), w_up[l].astype(BF16), w_down[l].astype(BF16)
        xs = _moe(xs, logits_t, h2, g2, wg, wu, wd, n=SEQ, sets=B, row0=0, mod_row=None)
        if l != L - 1:
            xs = _moe(xs, logits_t, h2, g2, wg, wu, wd, n=CTX, sets=B, row0=NL, mod_row=B)

    return _final_norm(xs, final_g, NL, tm=tm).reshape(B, SEQ, D)
```

```python
import functools

import jax
import jax.numpy as jnp
from jax import lax
from jax.experimental import pallas as pl
from jax.experimental.pallas import tpu as pltpu

F32 = jnp.float32
BF16 = jnp.bfloat16
I32 = jnp.int32

GRID_W = 64
N_HEADS = 16
Q_LORA = 1024
KV_LORA = 512
QK_NOPE = 128
QK_ROPE = 64
V_DIM = 128
ROPE_BASE = 10000.0
POOL_WINDOWS = (2, 4, 8, 16)
POOL_W = 1024
GM_CHUNK = 128
GM_GROUPS = 4
GM_W = 1024
N_EXPERTS = 16
CAPACITY_FACTOR = 2
EPS = 1e-6
LOG2_E = 1.4426950408889634

V7X_LANES = 128
V7X_VMEM_BYTES = 64 * 2**20
MIB = 2**20

ROW_TILE = 512
PROJ_COLS = 1024
NORM_CHUNK = 16
POOL_HALO = 64
HEADS_PER_STEP = 4
ATTN_ROWS = 16
ATTN_KEYS = 512
ATTN_QUERIES = 512
MOE_TILE = 256
CAST_BLOCK_BYTES = 8 * MIB
ROUTE_BLOCK = 128
DMA_WORDS = 128


def _params(semantics, vmem_mib):
    return pltpu.CompilerParams(dimension_semantics=semantics, vmem_limit_bytes=int(vmem_mib * MIB))


def _pick(n, pref):
    best = None
    for t in range(V7X_LANES, min(n, pref) + 1, V7X_LANES):
        if n % t == 0:
            best = t
    assert best is not None, (n, pref)
    return best


def _ada_kernel(c_ref, w_ref, b_ref, o_ref):
    c = c_ref[...]
    s = c * jax.nn.sigmoid(c)
    acc = jnp.dot(s.astype(BF16), w_ref[...].astype(BF16), preferred_element_type=F32)
    o_ref[...] = acc + b_ref[...]


def _ada(cc, w_ada, b_ada):
    L, D, N = w_ada.shape
    R = cc.shape[0]
    tn = _pick(N, 512)
    return pl.pallas_call(
        _ada_kernel,
        out_shape=jax.ShapeDtypeStruct((L, R, N), F32),
        grid=(L, N // tn),
        in_specs=[
            pl.BlockSpec((R, D), lambda l, j: (0, 0)),
            pl.BlockSpec((None, D, tn), lambda l, j: (l, 0, j)),
            pl.BlockSpec((None, 1, tn), lambda l, j: (l, 0, j)),
        ],
        out_specs=pl.BlockSpec((None, R, tn), lambda l, j: (l, 0, j)),
        compiler_params=_params(("arbitrary", "arbitrary"), 2 * 2 * D * tn * 4 / MIB + 8),
        name="ada",
    )(cc, w_ada, b_ada.reshape(L, 1, N))


def _norm_rows(x_ref, a, b, dst_refs, rows):
    def body(c, carry):
        r = pl.multiple_of(c * NORM_CHUNK, NORM_CHUNK)
        x = x_ref[pl.ds(r, NORM_CHUNK), :].astype(F32)
        ms = jnp.mean(x * x, axis=-1, keepdims=True)
        h = x * lax.rsqrt(ms + EPS) * a + b
        for d in dst_refs:
            d[pl.ds(r, NORM_CHUNK), :] = h.astype(d.dtype)
        return carry

    lax.fori_loop(0, rows // NORM_CHUNK, body, 0, unroll=2)


def _affine(a_ref, b_ref, mode):
    a = a_ref[...]
    if mode == "mod":
        a = 1.0 + a
    return a, b_ref[...]


def _in_proj_kernel(x_ref, sc_ref, sh_ref, w_ref, pa_ref, g_ref, h_ref, *, n_plain):
    j = pl.program_id(1)

    @pl.when(j == 0)
    def _():
        a, b = _affine(sc_ref, sh_ref, "mod")
        _norm_rows(x_ref, a, b, (h_ref,), x_ref.shape[0])

    acc = jnp.dot(h_ref[...], w_ref[...], preferred_element_type=F32)

    @pl.when(j < n_plain)
    def _():
        pa_ref[...] = acc

    @pl.when(j >= n_plain)
    def _():
        g_ref[...] = jax.nn.sigmoid(acc).astype(g_ref.dtype)


def _in_proj(x, sc, sh, w, *, n_plain, tm, tn, mod_fn):
    T, K = x.shape
    nj = w.shape[1] // tn
    vmem = (2 * tm * K * 4 + tm * K * 2 + 2 * K * tn * 2 + 2 * tm * tn * 6 + 2 * tm * tn * 4) / MIB + 6
    return pl.pallas_call(
        functools.partial(_in_proj_kernel, n_plain=n_plain),
        out_shape=(jax.ShapeDtypeStruct((T, n_plain * tn), F32),
                   jax.ShapeDtypeStruct((T, (nj - n_plain) * tn), BF16)),
        grid=(T // tm, nj),
        in_specs=[
            pl.BlockSpec((tm, K), lambda i, j: (i, 0)),
            pl.BlockSpec((None, 1, K), lambda i, j: (mod_fn(i), 0, 0)),
            pl.BlockSpec((None, 1, K), lambda i, j: (mod_fn(i), 0, 0)),
            pl.BlockSpec((K, tn), lambda i, j: (0, j)),
        ],
        out_specs=(pl.BlockSpec((tm, tn), lambda i, j: (i, jnp.minimum(j, n_plain - 1))),
                   pl.BlockSpec((tm, tn), lambda i, j: (i, jnp.maximum(j - n_plain, 0)))),
        scratch_shapes=[pltpu.VMEM((tm, K), BF16)],
        compiler_params=_params(("parallel", "arbitrary"), vmem),
        name="in_proj",
    )(x, sc, sh, w)


def _rope128(t, cos, sin):
    return t * cos + pltpu.roll(t, QK_ROPE, 1) * sin


def _q_kernel(x_ref, a_ref, b_ref, w_ref, cos_ref, sin_ref, o_ref, h_ref, *, scale):
    @pl.when(pl.program_id(1) == 0)
    def _():
        _norm_rows(x_ref, a_ref[...], b_ref[...], (h_ref,), x_ref.shape[0])

    acc = jnp.dot(h_ref[...], w_ref[...], preferred_element_type=F32)
    hq = o_ref.shape[2]
    for k in range(o_ref.shape[0]):
        rot = _rope128(acc[:, k * hq + QK_NOPE:(k + 1) * hq], cos_ref[...], sin_ref[...])
        o_ref[k, :, :QK_NOPE] = (acc[:, k * hq:k * hq + QK_NOPE] * scale).astype(o_ref.dtype)
        o_ref[k, :, QK_NOPE:] = (rot * scale).astype(o_ref.dtype)


def _kv_kernel(x_ref, a_ref, b_ref, w_ref, kr_ref, cos_ref, sin_ref, k_ref, v_ref, h_ref):
    @pl.when(pl.program_id(1) == 0)
    def _():
        _norm_rows(x_ref, a_ref[...], b_ref[...], (h_ref,), x_ref.shape[0])

    acc = jnp.dot(h_ref[...], w_ref[...], preferred_element_type=F32)
    rot = _rope128(kr_ref[...], cos_ref[...], sin_ref[...]).astype(k_ref.dtype)
    lane = lax.broadcasted_iota(I32, (v_ref.shape[1], V_DIM), 1)
    ones_col = jnp.where(lane == 0, 1.0, 0.0).astype(v_ref.dtype)
    hk = QK_NOPE + V_DIM
    for k in range(k_ref.shape[0]):
        k_ref[k, :, :QK_NOPE] = acc[:, k * hk:k * hk + QK_NOPE].astype(k_ref.dtype)
        k_ref[k, :, QK_NOPE:] = rot
        v_ref[k, :, :V_DIM] = acc[:, k * hk + QK_NOPE:(k + 1) * hk].astype(v_ref.dtype)
        v_ref[k, :, V_DIM:] = ones_col


def _lane_tile(x, width):
    return x if width == x.shape[1] else jnp.concatenate([x] * (width // x.shape[1]), axis=1)


def _attn_kernel(*refs, tk, n_lat):
    if n_lat:
        q_ref, kc_ref, vc_ref, kl_ref, vl_ref, o_ref, s_ref, p_ref, acc_ref, m_ref, al_ref = refs
    else:
        q_ref, kc_ref, vc_ref, o_ref, s_ref, p_ref, acc_ref, m_ref, al_ref = refs
    tq = q_ref.shape[0]
    ctx = kc_ref.shape[0]

    def scores(k):
        return lax.dot_general(q_ref[...], k, (((1,), (1,)), ((), ())), preferred_element_type=F32)

    def softmax(slot, width):
        for g in range(tq // ATTN_ROWS):
            rows = slice(g * ATTN_ROWS, (g + 1) * ATTN_ROWS)
            sg = s_ref[slot, rows, :width]
            m_old = m_ref[rows, :]
            m_new = jnp.maximum(m_old, jnp.max(sg, axis=-1, keepdims=True))
            p_ref[slot, rows, :width] = jnp.exp2(sg - _lane_tile(m_new, width)).astype(BF16)
            al_ref[slot, rows, :] = jnp.exp2(m_old - m_new)
            m_ref[rows, :] = m_new

    def accumulate(slot, width, v):
        acc_ref[...] = (_lane_tile(al_ref[slot], acc_ref.shape[1]) * acc_ref[...]
                        + jnp.dot(p_ref[slot, :, :width], v, preferred_element_type=F32))

    def lat(ref, c):
        if isinstance(c, int):
            return ref[c * tk:(c + 1) * tk, :]
        return ref[pl.ds(pl.multiple_of(c * tk, tk), tk), :]

    m_ref[...] = jnp.full(m_ref.shape, -jnp.inf, F32)
    acc_ref[...] = jnp.zeros(acc_ref.shape, F32)
    s_ref[0, :, :ctx] = scores(kc_ref[...])
    if n_lat:
        s_ref[1] = scores(kl_ref[0:tk, :])
    softmax(0, ctx)
    if n_lat:
        def step(c, odd, last):
            if not last:
                s_ref[1 if odd else 0] = scores(lat(kl_ref, jnp.minimum(c + 1, n_lat - 1)))
            softmax(0 if odd else 1, tk)
            accumulate(1 if odd else 0, tk, lat(vl_ref, c - 1))

        if n_lat > 1:
            s_ref[0] = scores(kl_ref[tk:2 * tk, :])
        softmax(1, tk)
        accumulate(0, ctx, vc_ref[...])
        pairs = (n_lat - 1) // 2

        def body(j, carry):
            step(1 + 2 * j, True, False)
            step(2 + 2 * j, False, False)
            return carry

        lax.fori_loop(0, pairs, body, 0)
        if (n_lat - 1) % 2:
            step(n_lat - 1, True, True)
        accumulate(n_lat % 2, tk, vl_ref[(n_lat - 1) * tk:n_lat * tk, :])
    else:
        accumulate(0, ctx, vc_ref[...])
    acc = acc_ref[...]
    o_ref[...] = (acc[:, :V_DIM] / acc[:, V_DIM:V_DIM + 1]).astype(o_ref.dtype)


def _attn_scratch(tq, tk, dv):
    return [pltpu.VMEM((2, tq, tk), F32), pltpu.VMEM((2, tq, tk), BF16), pltpu.VMEM((tq, dv), F32),
            pltpu.VMEM((tq, V7X_LANES), F32), pltpu.VMEM((2, tq, V7X_LANES), F32)]


def _attention(qc, kc, vc, *, B, SEQ, CTX):
    H, T, dq = qc.shape
    dv = vc.shape[2]
    NL = B * SEQ
    tq = min(ATTN_QUERIES, SEQ)
    tk = max(min(ATTN_KEYS, SEQ), CTX)
    assert SEQ % tk == 0 and SEQ % tq == 0 and tk >= CTX and tq % ATTN_ROWS == 0 and CTX % ATTN_ROWS == 0
    nq = SEQ // tq
    cblk = NL // CTX
    lat = pl.pallas_call(
        functools.partial(_attn_kernel, tk=tk, n_lat=SEQ // tk),
        out_shape=jax.ShapeDtypeStruct((NL, H * V_DIM), BF16),
        grid=(B, H, nq),
        in_specs=[
            pl.BlockSpec((None, tq, dq), lambda b, h, i: (h, b * nq + i, 0)),
            pl.BlockSpec((None, CTX, dq), lambda b, h, i: (h, cblk + b, 0)),
            pl.BlockSpec((None, CTX, dv), lambda b, h, i: (h, cblk + b, 0)),
            pl.BlockSpec((None, SEQ, dq), lambda b, h, i: (h, b, 0)),
            pl.BlockSpec((None, SEQ, dv), lambda b, h, i: (h, b, 0)),
        ],
        out_specs=pl.BlockSpec((tq, V_DIM), lambda b, h, i: (b * nq + i, h)),
        scratch_shapes=_attn_scratch(tq, tk, dv),
        compiler_params=_params(("parallel", "parallel", "arbitrary"),
                                2 * SEQ * (dq + dv) * 2 / MIB + 24),
        name="attn_latent",
    )(qc, kc, vc, kc, vc)
    ctx = pl.pallas_call(
        functools.partial(_attn_kernel, tk=tk, n_lat=0),
        out_shape=jax.ShapeDtypeStruct((B * CTX, H * V_DIM), BF16),
        grid=(B, H),
        in_specs=[
            pl.BlockSpec((None, CTX, dq), lambda b, h: (h, cblk + b, 0)),
            pl.BlockSpec((None, CTX, dq), lambda b, h: (h, cblk + b, 0)),
            pl.BlockSpec((None, CTX, dv), lambda b, h: (h, cblk + b, 0)),
        ],
        out_specs=pl.BlockSpec((CTX, V_DIM), lambda b, h: (b, h)),
        scratch_shapes=_attn_scratch(CTX, CTX, dv),
        compiler_params=_params(("parallel", "arbitrary"), 16),
        name="attn_ctx",
    )(qc, kc, vc)
    return jnp.concatenate([lat, ctx], axis=0)


def _pool_kernel(prev_ref, x_ref, next_ref, w_ref, ps_ref, o_ref, *, NL, SEQ, CTX):
    tm, width = x_ref.shape
    gw = width // len(POOL_WINDOWS)
    row0 = pl.program_id(0) * tm
    xm = x_ref[...]
    xcat = jnp.concatenate([prev_ref[...], xm, next_ref[...]], axis=0)
    hi_part = xcat.astype(BF16)
    lo_part = (xcat - hi_part.astype(F32)).astype(BF16)

    r = row0 + lax.broadcasted_iota(I32, (tm, 1), 0)
    s = row0 - POOL_HALO + lax.broadcasted_iota(I32, (1, tm + 2 * POOL_HALO), 1)
    lat_lo = (row0 // SEQ) * SEQ
    ctx_lo = jnp.full((tm, 1), row0, I32)
    for k in range(1, max(tm // CTX, 1)):
        ctx_lo = jnp.where(r >= row0 + k * CTX, row0 + k * CTX, ctx_lo)
    is_lat = row0 < NL
    seq_lo = jnp.where(is_lat, lat_lo, ctx_lo)
    seq_hi = seq_lo + jnp.where(is_lat, SEQ, CTX)

    for g, win in enumerate(POOL_WINDOWS):
        left = win // 2
        right = win - 1 - left
        lo = jnp.maximum(r - left, seq_lo)
        hi = jnp.minimum(r + right + 1, seq_hi)
        band = jnp.where((s >= lo) & (s < hi), 1.0, 0.0).astype(BF16)
        cols = slice(g * gw, (g + 1) * gw)
        tot = (jnp.dot(band, hi_part[:, cols], preferred_element_type=F32)
               + jnp.dot(band, lo_part[:, cols], preferred_element_type=F32))
        d = tot / (hi - lo).astype(F32) - xm[:, cols]
        y = jnp.dot(d.astype(BF16), w_ref[g], preferred_element_type=F32)
        o_ref[:, cols] = (y * ps_ref[:, cols]).astype(o_ref.dtype)


def _pool(pa, col, w_pool, pool_scale, *, tm, NL, SEQ, CTX):
    T = pa.shape[0]
    hb = tm // POOL_HALO
    last = T // POOL_HALO - 1
    return pl.pallas_call(
        functools.partial(_pool_kernel, NL=NL, SEQ=SEQ, CTX=CTX),
        out_shape=jax.ShapeDtypeStruct((T, POOL_W), BF16),
        grid=(T // tm,),
        in_specs=[
            pl.BlockSpec((POOL_HALO, POOL_W), lambda i: (jnp.maximum(i * hb - 1, 0), col)),
            pl.BlockSpec((tm, POOL_W), lambda i: (i, col)),
            pl.BlockSpec((POOL_HALO, POOL_W), lambda i: (jnp.minimum((i + 1) * hb, last), col)),
            pl.BlockSpec(w_pool.shape, lambda i: (0, 0, 0)),
            pl.BlockSpec((1, POOL_W), lambda i: (0, 0)),
        ],
        out_specs=pl.BlockSpec((tm, POOL_W), lambda i: (i, 0)),
        compiler_params=_params(("parallel",), 40),
        name="pool",
    )(pa, pa, pa, w_pool, pool_scale)


def _gelu(z):
    return 0.5 * z * (1.0 + lax.erf(z * (2.0 ** -0.5)))


def _gm_kernel(u_ref, v_ref, ws_ref, bias_ref, o_ref):
    tm = u_ref.shape[0]
    gw = GM_W // GM_GROUPS

    def body(c, carry):
        r = pl.multiple_of(c * GM_CHUNK, GM_CHUNK)
        zu = _gelu(u_ref[pl.ds(r, GM_CHUNK), :])
        zv = _gelu(v_ref[pl.ds(r, GM_CHUNK), :])
        mu = jnp.mean(zv, axis=-1, keepdims=True)
        var = jnp.mean(jnp.square(zv - mu), axis=-1, keepdims=True)
        vn = ((zv - mu) * lax.rsqrt(var + EPS)).astype(BF16)
        for g in range(GM_GROUPS):
            cols = slice(g * gw, (g + 1) * gw)
            sv = jnp.dot(ws_ref[g], vn[:, cols], preferred_element_type=F32) + bias_ref[:, cols]
            o_ref[pl.ds(r, GM_CHUNK), cols] = (zu[:, cols] * sv).astype(o_ref.dtype)
        return carry

    lax.fori_loop(0, tm // GM_CHUNK, body, 0)


def _gm(pa, ucol, w_s, bias_full, *, tm):
    T = pa.shape[0]
    return pl.pallas_call(
        _gm_kernel,
        out_shape=jax.ShapeDtypeStruct((T, GM_W), BF16),
        grid=(T // tm,),
        in_specs=[
            pl.BlockSpec((tm, GM_W), lambda i: (i, ucol)),
            pl.BlockSpec((tm, GM_W), lambda i: (i, ucol + 1)),
            pl.BlockSpec(w_s.shape, lambda i: (0, 0, 0)),
            pl.BlockSpec(bias_full.shape, lambda i: (0, 0)),
        ],
        out_specs=pl.BlockSpec((tm, GM_W), lambda i: (i, 0)),
        compiler_params=_params(("parallel",), 32),
        name="gmlp",
    )(pa, pa, w_s, bias_full)


def _branch_kernel(at_ref, po_ref, gm_ref, w_ref, g0_ref, g1_ref, g2_ref, o_ref):
    ka = at_ref.shape[1]
    kp = po_ref.shape[1]
    y = g0_ref[...].astype(F32) * jnp.dot(at_ref[...], w_ref[0:ka, :], preferred_element_type=F32)
    y += g1_ref[...].astype(F32) * jnp.dot(po_ref[...], w_ref[ka:ka + kp, :], preferred_element_type=F32)
    y += g2_ref[...].astype(F32) * jnp.dot(gm_ref[...], w_ref[ka + kp:, :], preferred_element_type=F32)
    o_ref[...] = y.astype(o_ref.dtype)


def _branch(attn, pool, gm, w_branch, gates, *, tm, tn):
    T = attn.shape[0]
    K, D = w_branch.shape
    nj = D // tn
    return pl.pallas_call(
        _branch_kernel,
        out_shape=jax.ShapeDtypeStruct((T, D), BF16),
        grid=(T // tm, nj),
        in_specs=[
            pl.BlockSpec((tm, attn.shape[1]), lambda i, j: (i, 0)),
            pl.BlockSpec((tm, pool.shape[1]), lambda i, j: (i, 0)),
            pl.BlockSpec((tm, gm.shape[1]), lambda i, j: (i, 0)),
            pl.BlockSpec((K, tn), lambda i, j: (0, j)),
            pl.BlockSpec((tm, tn), lambda i, j: (i, j)),
            pl.BlockSpec((tm, tn), lambda i, j: (i, nj + j)),
            pl.BlockSpec((tm, tn), lambda i, j: (i, 2 * nj + j)),
        ],
        out_specs=pl.BlockSpec((tm, tn), lambda i, j: (i, j)),
        compiler_params=_params(("parallel", "arbitrary"),
                                (2 * tm * K * 2 + 2 * K * tn * 2 + 8 * tm * tn * 2 + 4 * tm * tn * 4) / MIB + 8),
        name="branch_mix",
    )(attn, pool, gm, w_branch, gates, gates, gates)


def _mm_res_kernel(a_ref, w_ref, r_ref, g_ref, o_ref):
    acc = jnp.dot(a_ref[...], w_ref[...], preferred_element_type=F32)
    o_ref[...] = r_ref[...] + g_ref[...] * acc


def _mm_res(a, w, res, gate, *, tm, tn, mod_fn):
    T, K = a.shape
    N = w.shape[1]
    return pl.pallas_call(
        _mm_res_kernel,
        out_shape=jax.ShapeDtypeStruct((T, N), F32),
        grid=(T // tm, N // tn),
        in_specs=[
            pl.BlockSpec((tm, K), lambda i, j: (i, 0)),
            pl.BlockSpec((K, tn), lambda i, j: (0, j)),
            pl.BlockSpec((tm, tn), lambda i, j: (i, j)),
            pl.BlockSpec((None, 1, tn), lambda i, j: (mod_fn(i), 0, j)),
        ],
        out_specs=pl.BlockSpec((tm, tn), lambda i, j: (i, j)),
        input_output_aliases={2: 0},
        compiler_params=_params(("parallel", "arbitrary"),
                                (2 * tm * K * 2 + 2 * K * tn * 2 + 6 * tm * tn * 4) / MIB + 8),
        name="out_proj",
    )(a, w, res, gate)


def _mod_router_kernel(x_ref, sc_ref, sh_ref, wr_ref, h_ref, lg_ref, hb_ref):
    a, b = _affine(sc_ref, sh_ref, "mod")
    _norm_rows(x_ref, a, b, (h_ref, hb_ref), x_ref.shape[0])
    lg_ref[...] = lax.dot_general(wr_ref[...], hb_ref[...], (((1,), (1,)), ((), ())),
                                  preferred_element_type=F32)


def _mod_router(xs, sc, sh, wr_t, *, tm, mod_fn):
    T, D = xs.shape
    E = wr_t.shape[0]
    return pl.pallas_call(
        _mod_router_kernel,
        out_shape=(jax.ShapeDtypeStruct((T, D), F32), jax.ShapeDtypeStruct((E, T), F32)),
        grid=(T // tm,),
        in_specs=[
            pl.BlockSpec((tm, D), lambda i: (i, 0)),
            pl.BlockSpec((None, 1, D), lambda i: (mod_fn(i), 0, 0)),
            pl.BlockSpec((None, 1, D), lambda i: (mod_fn(i), 0, 0)),
            pl.BlockSpec((E, D), lambda i: (0, 0)),
        ],
        out_specs=(pl.BlockSpec((tm, D), lambda i: (i, 0)), pl.BlockSpec((E, tm), lambda i: (0, i))),
        scratch_shapes=[pltpu.VMEM((tm, D), BF16)],
        compiler_params=_params(("parallel",), (4 * tm * D * 4 + tm * D * 2) / MIB + 8),
        name="mod_router",
    )(xs, sc, sh, wr_t)


def _route_kernel(lg_ref, li_ref, lg_out_ref, cnt_ref, code_ref, *, cap):
    lg = lg_ref[...]
    E, n = lg.shape
    mx = jnp.max(lg, axis=0, keepdims=True)
    ex = jnp.exp(lg - mx)
    aff = ex / jnp.sum(ex, axis=0, keepdims=True)
    bits = pltpu.bitcast(aff, I32)

    def count(mask):
        return jnp.sum(jnp.where(mask, 1.0, 0.0), axis=1, keepdims=True)

    v = jnp.zeros((E, 1), I32)
    for bit in range(30, -1, -1):
        cand = v | (1 << bit)
        v = jnp.where(count(bits >= cand) >= cap, cand, v)
    gt = bits > v
    eq = bits == v
    need = cap - count(gt)
    t = lax.broadcasted_iota(I32, (E, n), 1)
    u = jnp.zeros((E, 1), I32)
    for bit in range(max(n - 1, 1).bit_length() - 1, -1, -1):
        cand = u | (1 << bit)
        u = jnp.where(count(eq & (t < cand)) < need, cand, u)
    sel = gt | (eq & (t <= u))
    code_ref[...] = jnp.where(sel, bits | jnp.int32(-2**31), bits)

    W = ROUTE_BLOCK
    sub = lax.broadcasted_iota(I32, (W, W), 0)
    lane_sq = lax.broadcasted_iota(I32, (W, W), 1)
    below = jnp.where(sub < lane_sq, 1.0, 0.0).astype(BF16)
    slot = sub.astype(F32)
    lane = lax.broadcasted_iota(I32, (E, W), 1)
    head = jnp.where(lax.broadcasted_iota(I32, (E, W), 0) == 0, lane.astype(F32), 0.0).astype(BF16)

    def block(k, cnt_acc):
        off = pl.multiple_of(k * W, W)
        w = code_ref[:, pl.ds(off, W)]
        picked = w < 0
        a = pltpu.bitcast(w & 0x7FFFFFFF, F32)
        ones = jnp.where(picked, 1.0, 0.0)
        pos = jnp.dot(ones.astype(BF16), below, preferred_element_type=F32)
        pos = jnp.where(picked, pos, -1.0)
        hi = a.astype(BF16)
        r1 = a - hi.astype(F32)
        mid = r1.astype(BF16)
        lo = (r1 - mid.astype(F32)).astype(BF16)
        vals = jnp.concatenate([head, hi, mid, lo], axis=0)
        for e in range(E):
            onehot = jnp.where(slot == jnp.broadcast_to(pos[e:e + 1, :], (W, W)), 1.0, 0.0).astype(BF16)
            c = lax.dot_general(vals, onehot, (((1,), (1,)), ((), ())), preferred_element_type=F32)
            li_ref[e:e + 1, pl.ds(off, W)] = c[0:1, :].astype(I32) + k * W
            lg_out_ref[e:e + 1, pl.ds(off, W)] = (c[E + e:E + e + 1, :] + c[2 * E + e:2 * E + e + 1, :]
                                                   + c[3 * E + e:3 * E + e + 1, :])
        return cnt_acc + jnp.where(lane == k, jnp.sum(ones, axis=1, keepdims=True), 0.0)

    cnt = lax.fori_loop(0, n // W, block, jnp.zeros((E, W), F32))
    cnt_ref[...] = cnt.astype(I32)


def _route(logits_t, *, n, cap, blk0, sets):
    E = logits_t.shape[0]
    assert n % ROUTE_BLOCK == 0 and n // ROUTE_BLOCK <= ROUTE_BLOCK and E % 16 == 0
    return pl.pallas_call(
        functools.partial(_route_kernel, cap=cap),
        out_shape=(jax.ShapeDtypeStruct((sets, E, n), I32), jax.ShapeDtypeStruct((sets, E, n), F32),
                   jax.ShapeDtypeStruct((sets, E, ROUTE_BLOCK), I32)),
        grid=(sets,),
        in_specs=[pl.BlockSpec((E, n), lambda s: (0, blk0 + s))],
        out_specs=(pl.BlockSpec((None, E, n), lambda s: (s, 0, 0)), pl.BlockSpec((None, E, n), lambda s: (s, 0, 0)),
                   pl.BlockSpec((None, E, ROUTE_BLOCK), lambda s: (s, 0, 0))),
        scratch_shapes=[pltpu.VMEM((E, n), I32)],
        compiler_params=_params(("parallel",), 32),
        name="route_select",
    )(logits_t)


def _compact_kernel(li_hbm, lg_hbm, cnt_hbm, idx_hbm, gate_hbm, li_s, lg_s, cnt_s, idx_s, gate_s, sem_in, sem_out,
                    *, n, cap):
    s = pl.program_id(0)
    e = pl.program_id(1)
    loads = [pltpu.make_async_copy(li_hbm.at[s, e], li_s, sem_in.at[0]),
             pltpu.make_async_copy(lg_hbm.at[s, e], lg_s, sem_in.at[1]),
             pltpu.make_async_copy(cnt_hbm.at[s, e], cnt_s, sem_in.at[2])]
    for cp in loads:
        cp.start()
    for cp in loads:
        cp.wait()

    def block(k, c0):
        nk = cnt_s[k]

        def entry(j, carry):
            idx_s[c0 + j] = li_s[k * ROUTE_BLOCK + j]
            gate_s[c0 + j] = lg_s[k * ROUTE_BLOCK + j]
            return carry

        lax.fori_loop(0, nk, entry, 0)
        return c0 + nk

    lax.fori_loop(0, n // ROUTE_BLOCK, block, jnp.int32(0))
    cap_pad = idx_hbm.shape[-1]

    def clear(p, carry):
        idx_s[p] = 0
        gate_s[p] = 0.0
        return carry

    lax.fori_loop(cap, cap_pad, clear, 0)
    stores = [pltpu.make_async_copy(idx_s, idx_hbm.at[s, e], sem_out.at[0]),
              pltpu.make_async_copy(gate_s, gate_hbm.at[s, e], sem_out.at[1])]
    for cp in stores:
        cp.start()
    for cp in stores:
        cp.wait()


def _compact(li, lg, cnt, *, cap):
    sets, E, n = li.shape
    cap_pad = -(-cap // DMA_WORDS) * DMA_WORDS
    idx, gates = pl.pallas_call(
        functools.partial(_compact_kernel, n=n, cap=cap),
        out_shape=(jax.ShapeDtypeStruct((sets, E, cap_pad), I32), jax.ShapeDtypeStruct((sets, E, cap_pad), F32)),
        grid=(sets, E),
        in_specs=[pl.BlockSpec(memory_space=pl.ANY)] * 3,
        out_specs=(pl.BlockSpec(memory_space=pl.ANY), pl.BlockSpec(memory_space=pl.ANY)),
        scratch_shapes=[pltpu.SMEM((n,), I32), pltpu.SMEM((n,), F32), pltpu.SMEM((ROUTE_BLOCK,), I32),
                        pltpu.SMEM((cap_pad,), I32), pltpu.SMEM((cap_pad,), F32),
                        pltpu.SemaphoreType.DMA((3,)), pltpu.SemaphoreType.DMA((2,))],
        compiler_params=_params(("arbitrary", "arbitrary"), 16),
        name="route_compact",
    )(li, lg, cnt)
    return idx[:, :, :cap], gates[:, :, :cap]


def _row_copy(hbm, vmem, slot, r, row, sem, to_hbm):
    h, v = hbm.at[pl.ds(row, 1)], vmem.at[slot, pl.ds(r, 1)]
    return pltpu.make_async_copy(v, h, sem.at[slot]) if to_hbm else pltpu.make_async_copy(h, v, sem.at[slot])


def _row_starts(hbm, vmem, idx_s, slot, sem, rows, *, to_hbm=False):
    for r in range(rows):
        _row_copy(hbm, vmem, slot, r, idx_s[slot, r], sem, to_hbm).start()


def _row_waits(hbm, vmem, slot, sem, rows, *, to_hbm=False):
    def body(r, carry):
        _row_copy(hbm, vmem, slot, r, 0, sem, to_hbm).wait()
        return carry

    lax.fori_loop(0, rows, body, 0, unroll=8)


def _idx_copy(idx_hbm, idx_s, slot, sem, step):
    return pltpu.make_async_copy(idx_hbm.at[step], idx_s.at[slot], sem)


def _step_ids():
    step = pl.program_id(0) * pl.num_programs(1) + pl.program_id(1)
    return step, pl.num_programs(0) * pl.num_programs(1), lax.rem(step, 2)


def _ffn1_kernel(idx_hbm, h_hbm, wg_ref, wu_ref, o_ref, xbuf, idx_s, sem_i, sem_g):
    tm = xbuf.shape[1]
    step, total, slot = _step_ids()

    def fetch(st, sl):
        cp = _idx_copy(idx_hbm, idx_s, sl, sem_i, st)
        cp.start()
        cp.wait()
        _row_starts(h_hbm, xbuf, idx_s, sl, sem_g, tm)

    @pl.when(step == 0)
    def _():
        fetch(0, 0)

    @pl.when(step + 1 < total)
    def _():
        fetch(step + 1, 1 - slot)

    _row_waits(h_hbm, xbuf, slot, sem_g, tm)
    xb = xbuf[slot].astype(BF16)
    a = jnp.dot(xb, wg_ref[...], preferred_element_type=F32)
    u = jnp.dot(xb, wu_ref[...], preferred_element_type=F32)
    o_ref[...] = (a * jax.nn.sigmoid(a) * u).astype(o_ref.dtype)


def _ffn1(idx_tiles, h2, w_gate, w_up, *, tm):
    E, D, F = w_gate.shape
    tiles = idx_tiles.shape[0] // E
    return pl.pallas_call(
        _ffn1_kernel,
        out_shape=jax.ShapeDtypeStruct((E, tiles * tm, F), BF16),
        grid=(E, tiles),
        in_specs=[
            pl.BlockSpec(memory_space=pl.ANY),
            pl.BlockSpec(memory_space=pl.ANY),
            pl.BlockSpec((None, D, F), lambda e, m: (e, 0, 0)),
            pl.BlockSpec((None, D, F), lambda e, m: (e, 0, 0)),
        ],
        out_specs=pl.BlockSpec((None, tm, F), lambda e, m: (e, m, 0)),
        scratch_shapes=[pltpu.VMEM((2, tm, D), F32), pltpu.SMEM((2, max(tm, DMA_WORDS)), I32),
                        pltpu.SemaphoreType.DMA(()), pltpu.SemaphoreType.DMA((2,))],
        compiler_params=_params(("arbitrary", "arbitrary"),
                                (4 * D * F * 2 + tm * D * 10 + 4 * tm * F * 4) / MIB + 6),
        name="expert_ffn_in",
    )(idx_tiles, h2, w_gate, w_up)


def _ffn2_kernel(idx_hbm, hh_ref, wd_ref, gate_ref, g2_ref, acc_in, o_hbm, abuf, idx_s, sem_i, sem_g, sem_s,
                 *, overlap):
    del acc_in
    tm = abuf.shape[1]
    step, total, slot = _step_ids()

    def fetch(st, sl):
        cp = _idx_copy(idx_hbm, idx_s, sl, sem_i, st)
        cp.start()
        cp.wait()
        _row_starts(o_hbm, abuf, idx_s, sl, sem_g, tm)

    if overlap:
        @pl.when(step == 0)
        def _():
            fetch(0, 0)
    else:
        fetch(step, slot)

    y = jnp.dot(hh_ref[...], wd_ref[...], preferred_element_type=F32)
    y = y * gate_ref[...] * g2_ref[...]
    _row_waits(o_hbm, abuf, slot, sem_g, tm)
    abuf[slot] = abuf[slot] + y
    _row_starts(o_hbm, abuf, idx_s, slot, sem_s, tm, to_hbm=True)

    if overlap:
        @pl.when(step > 0)
        def _():
            _row_waits(o_hbm, abuf, 1 - slot, sem_s, tm, to_hbm=True)

        @pl.when(step + 1 < total)
        def _():
            fetch(step + 1, 1 - slot)

        @pl.when(step + 1 == total)
        def _():
            _row_waits(o_hbm, abuf, slot, sem_s, tm, to_hbm=True)
    else:
        _row_waits(o_hbm, abuf, slot, sem_s, tm, to_hbm=True)


def _ffn2_scatter(idx_tiles, hh, w_down, gates, g2, xs, *, tm, mod_fn, overlap):
    E, F, D = w_down.shape
    tiles = idx_tiles.shape[0] // E
    return pl.pallas_call(
        functools.partial(_ffn2_kernel, overlap=overlap),
        out_shape=jax.ShapeDtypeStruct(xs.shape, xs.dtype),
        grid=(E, tiles),
        in_specs=[
            pl.BlockSpec(memory_space=pl.ANY),
            pl.BlockSpec((None, tm, F), lambda e, m: (e, m, 0)),
            pl.BlockSpec((None, F, D), lambda e, m: (e, 0, 0)),
            pl.BlockSpec((None, tm, 1), lambda e, m: (e, m, 0)),
            pl.BlockSpec((None, 1, D), lambda e, m: (mod_fn(m), 0, 0)),
            pl.BlockSpec(memory_space=pl.ANY),
        ],
        out_specs=pl.BlockSpec(memory_space=pl.ANY),
        scratch_shapes=[pltpu.VMEM((2, tm, D), F32), pltpu.SMEM((2, max(tm, DMA_WORDS)), I32),
                        pltpu.SemaphoreType.DMA(()), pltpu.SemaphoreType.DMA((2,)), pltpu.SemaphoreType.DMA((2,))],
        input_output_aliases={5: 0},
        compiler_params=_params(("arbitrary", "arbitrary"),
                                (2 * F * D * 2 + 4 * tm * D * 4 + 4 * tm * F * 2) / MIB + 8),
        name="expert_ffn_out",
    )(idx_tiles, hh, w_down, gates, g2, xs)


def _moe(xs, logits_t, h2, g2, w_gate, w_up, w_down, *, n, sets, row0, mod_row):
    E = logits_t.shape[0]
    cap = (CAPACITY_FACTOR * n) // E
    idx, aff = _compact(*_route(logits_t, n=n, cap=cap, blk0=row0 // n, sets=sets), cap=cap)
    tm = min(MOE_TILE, cap if mod_row is None else sets * cap)
    assert (sets * cap) % tm == 0 and (mod_row is not None or cap % tm == 0)
    rows = idx + (row0 + n * jnp.arange(sets, dtype=I32))[:, None, None]
    idx_tiles = jnp.transpose(rows, (1, 0, 2)).reshape(E * sets * cap // tm, tm)
    idx_tiles = jnp.pad(idx_tiles, ((0, 0), (0, max(DMA_WORDS - tm, 0))))
    gates = jnp.transpose(aff, (1, 0, 2)).reshape(E, sets * cap, 1)
    hh = _ffn1(idx_tiles, h2, w_gate, w_up, tm=tm)
    mod_fn = (lambda m: m // (cap // tm)) if mod_row is None else (lambda m: mod_row)
    overlap = mod_row is None and sets >= 2
    return _ffn2_scatter(idx_tiles, hh, w_down, gates, g2, xs, tm=tm, mod_fn=mod_fn, overlap=overlap)


def _cast_kernel(x_ref, o_ref):
    o_ref[...] = x_ref[...].astype(o_ref.dtype)


def _layer_bf16(w, l):
    L, shape = w.shape[0], w.shape[1:]
    C = shape[-1]
    R = 1
    for d in shape[:-1]:
        R *= d
    br = R
    while br * C * 4 > CAST_BLOCK_BYTES and br % 16 == 0:
        br //= 2
    nb = R // br
    out = pl.pallas_call(
        _cast_kernel,
        out_shape=jax.ShapeDtypeStruct((R, C), BF16),
        grid=(nb,),
        in_specs=[pl.BlockSpec((br, C), lambda i: (l * nb + i, 0))],
        out_specs=pl.BlockSpec((br, C), lambda i: (i, 0)),
        compiler_params=_params(("parallel",), 2 * br * C * 6 / MIB + 4),
        name="weight_cast",
    )(w.reshape(L * R, C))
    return out.reshape(shape)


def _final_norm_kernel(x_ref, g_ref, z_ref, o_ref):
    _norm_rows(x_ref, g_ref[...], z_ref[...], (o_ref,), x_ref.shape[0])


def _final_norm(xs, gain, rows, *, tm):
    D = xs.shape[1]
    return pl.pallas_call(
        _final_norm_kernel,
        out_shape=jax.ShapeDtypeStruct((rows, D), F32),
        grid=(rows // tm,),
        in_specs=[
            pl.BlockSpec((tm, D), lambda i: (i, 0)),
            pl.BlockSpec((1, D), lambda i: (0, 0)),
            pl.BlockSpec((1, D), lambda i: (0, 0)),
        ],
        out_specs=pl.BlockSpec((tm, D), lambda i: (i, 0)),
        compiler_params=_params(("parallel",), 4 * tm * D * 4 / MIB + 8),
        name="final_norm",
    )(xs, gain.reshape(1, D), jnp.zeros((1, D), F32))


def _rope_tables(B, SEQ, CTX):
    rows = SEQ // GRID_W
    row = jnp.broadcast_to(jnp.arange(rows, dtype=F32)[:, None], (rows, GRID_W)).reshape(-1)
    col = jnp.broadcast_to(jnp.arange(GRID_W, dtype=F32)[None, :], (rows, GRID_W)).reshape(-1)
    n_freq = QK_ROPE // 4
    inv = ROPE_BASE ** (-jnp.arange(n_freq, dtype=F32) / n_freq)
    ar = row[:, None] * inv
    ac = col[:, None] * inv
    cos = jnp.concatenate([jnp.cos(ar), jnp.cos(ar), jnp.cos(ac), jnp.cos(ac)], axis=-1)
    sin = jnp.concatenate([-jnp.sin(ar), jnp.sin(ar), -jnp.sin(ac), jnp.sin(ac)], axis=-1)
    pad = jnp.zeros((SEQ, 2 * QK_ROPE - QK_ROPE), F32)
    cos_l = jnp.concatenate([cos, pad], axis=-1)
    sin_l = jnp.concatenate([sin, pad], axis=-1)
    cos_c = jnp.concatenate([jnp.ones((B * CTX, QK_ROPE), F32), jnp.zeros((B * CTX, QK_ROPE), F32)], axis=-1)
    cos_t = jnp.concatenate([jnp.tile(cos_l, (B, 1)), cos_c], axis=0)
    sin_t = jnp.concatenate([jnp.tile(sin_l, (B, 1)), jnp.zeros((B * CTX, 2 * QK_ROPE), F32)], axis=0)
    return cos_t, sin_t


def _swap_perm():
    q = QK_ROPE // 4
    return jnp.concatenate([jnp.arange(q, 2 * q), jnp.arange(0, q), jnp.arange(3 * q, 4 * q), jnp.arange(2 * q, 3 * q)])


def kernel(x, c, ctx, c_ctx, w_ada, b_ada, w_in, q_norm_g, w_uq, kv_norm_g, w_ukv, w_pool, pool_scale,
           w_spatial, b_spatial, w_branch, w_out, w_router, w_gate, w_up, w_down, final_g):
    B, SEQ, D = x.shape
    CTX = ctx.shape[1]
    L = w_ada.shape[0]
    H = N_HEADS
    NL, NC = B * SEQ, B * CTX
    T = NL + NC
    tm = min(ROW_TILE, NC)
    assert SEQ % tm == 0 and NC % tm == 0 and tm % CTX == 0 and CTX % GM_CHUNK == 0 and SEQ % GM_CHUNK == 0
    assert QK_NOPE == V7X_LANES and 2 * QK_ROPE == V7X_LANES and V_DIM == V7X_LANES
    assert Q_LORA == POOL_W == GM_W and Q_LORA % KV_LORA == 0 and KV_LORA % V7X_LANES == 0
    tiles_per_seq = SEQ // tm

    def mod_tile(i):
        return jnp.minimum(i // tiles_per_seq, B)

    kv_start = Q_LORA
    kr_start = kv_start + KV_LORA
    pool_start = kr_start + QK_ROPE
    gm_start = pool_start + POOL_W
    gate_start = gm_start + 2 * GM_W
    wa_cols = Q_LORA + POOL_W + 2 * GM_W + KV_LORA + 2 * QK_ROPE
    tn_in = _pick(3 * D, PROJ_COLS)
    wa_pad = -wa_cols % tn_in
    perm = _swap_perm()
    dh = QK_NOPE + QK_ROPE
    scale = float(dh) ** -0.5 * LOG2_E

    cos_t, sin_t = _rope_tables(B, SEQ, CTX)
    xs = jnp.concatenate([x.reshape(NL, D), ctx.reshape(NC, D)], axis=0)

    cc = jnp.zeros((8, D), F32).at[:B].set(c).at[B].set(c_ctx)
    mods = _ada(cc, w_ada, b_ada)[:, :B + 1].reshape(L, B + 1, 6, 1, D)

    zeros_q = jnp.zeros((1, 1, Q_LORA), F32)
    zeros_kv = jnp.zeros((1, 1, KV_LORA), F32)

    for l in range(L):
        sh1, sc1, g1, sh2, sc2, g2 = (mods[l, :, k] for k in range(6))
        wl = w_in[l]
        kr_w = wl[:, kr_start:kr_start + QK_ROPE]
        w_all = jnp.concatenate(
            [wl[:, :Q_LORA], wl[:, pool_start:pool_start + POOL_W], wl[:, gm_start:gm_start + 2 * GM_W],
             wl[:, kv_start:kv_start + KV_LORA], kr_w, kr_w[:, perm], jnp.zeros((D, wa_pad), F32),
             wl[:, gate_start:]], axis=1).astype(BF16)
        uq = w_uq[l].reshape(Q_LORA, H, dh)
        uq_ext = jnp.concatenate([uq, uq[:, :, QK_NOPE:][:, :, perm]], axis=-1).reshape(Q_LORA, H * (dh + QK_ROPE))
        uq_ext = uq_ext.astype(BF16)
        ukv = w_ukv[l].astype(BF16)

        pa, gates = _in_proj(xs, sc1, sh1, w_all, n_plain=(wa_cols + wa_pad) // tn_in, tm=tm, tn=tn_in,
                             mod_fn=mod_tile)
        col_pool = Q_LORA // POOL_W
        col_gm = (Q_LORA + POOL_W) // GM_W
        col_kv = (Q_LORA + POOL_W + 2 * GM_W) // KV_LORA
        col_kr = (Q_LORA + POOL_W + 2 * GM_W + KV_LORA) // V7X_LANES

        hq = dh + QK_ROPE
        hb = min(HEADS_PER_STEP, H)
        qc = pl.pallas_call(
            functools.partial(_q_kernel, scale=scale),
            out_shape=jax.ShapeDtypeStruct((H, T, hq), BF16),
            grid=(T // tm, H // hb),
            in_specs=[
                pl.BlockSpec((tm, Q_LORA), lambda i, h: (i, 0)),
                pl.BlockSpec((None, 1, Q_LORA), lambda i, h: (0, 0, 0)),
                pl.BlockSpec((None, 1, Q_LORA), lambda i, h: (0, 0, 0)),
                pl.BlockSpec((Q_LORA, hb * hq), lambda i, h: (0, h)),
                pl.BlockSpec((tm, 2 * QK_ROPE), lambda i, h: (i, 0)),
                pl.BlockSpec((tm, 2 * QK_ROPE), lambda i, h: (i, 0)),
            ],
            out_specs=pl.BlockSpec((hb, tm, hq), lambda i, h: (h, i, 0)),
            scratch_shapes=[pltpu.VMEM((tm, Q_LORA), BF16)],
            compiler_params=_params(("parallel", "arbitrary"), 24),
            name="q_proj",
        )(pa, q_norm_g[l].reshape(1, 1, Q_LORA), zeros_q, uq_ext, cos_t, sin_t)
        kc, vc = pl.pallas_call(
            _kv_kernel,
            out_shape=(jax.ShapeDtypeStruct((H, T, hq), BF16), jax.ShapeDtypeStruct((H, T, 2 * V_DIM), BF16)),
            grid=(T // tm, H // hb),
            in_specs=[
                pl.BlockSpec((tm, KV_LORA), lambda i, h: (i, col_kv)),
                pl.BlockSpec((None, 1, KV_LORA), lambda i, h: (0, 0, 0)),
                pl.BlockSpec((None, 1, KV_LORA), lambda i, h: (0, 0, 0)),
                pl.BlockSpec((KV_LORA, hb * (QK_NOPE + V_DIM)), lambda i, h: (0, h)),
                pl.BlockSpec((tm, 2 * QK_ROPE), lambda i, h: (i, col_kr)),
                pl.BlockSpec((tm, 2 * QK_ROPE), lambda i, h: (i, 0)),
                pl.BlockSpec((tm, 2 * QK_ROPE), lambda i, h: (i, 0)),
            ],
            out_specs=(pl.BlockSpec((hb, tm, hq), lambda i, h: (h, i, 0)),
                       pl.BlockSpec((hb, tm, 2 * V_DIM), lambda i, h: (h, i, 0))),
            scratch_shapes=[pltpu.VMEM((tm, KV_LORA), BF16)],
            compiler_params=_params(("parallel", "arbitrary"), 24),
            name="kv_proj",
        )(pa, kv_norm_g[l].reshape(1, 1, KV_LORA), zeros_kv, ukv, pa, cos_t, sin_t)
        attn = _attention(qc, kc, vc, B=B, SEQ=SEQ, CTX=CTX)

        pool = _pool(pa, col_pool, w_pool[l].astype(BF16), pool_scale[l].reshape(1, POOL_W),
                     tm=tm, NL=NL, SEQ=SEQ, CTX=CTX)
        bias_full = jnp.repeat(jnp.transpose(b_spatial[l]), GM_W // GM_GROUPS, axis=1)
        gm = _gm(pa, col_gm, w_spatial[l].astype(BF16), bias_full, tm=tm)

        y = _branch(attn, pool, gm, _layer_bf16(w_branch, l), gates, tm=tm, tn=_pick(D, PROJ_COLS))
        xs = _mm_res(y, _layer_bf16(w_out, l), xs, g1, tm=tm, tn=_pick(D, PROJ_COLS), mod_fn=mod_tile)

        h2, logits_t = _mod_router(xs, sc2, sh2, jnp.transpose(w_router[l]).astype(BF16), tm=tm, mod_fn=mod_tile)
        wg, wu, wd = _layer_bf16(w_gate, l), _layer_bf16(w_up, l), _layer_bf16(w_down, l)
        xs = _moe(xs, logits_t, h2, g2, wg, wu, wd, n=SEQ, sets=B, row0=0, mod_row=None)
        if l != L - 1:
            xs = _moe(xs, logits_t, h2, g2, wg, wu, wd, n=CTX, sets=B, row0=NL, mod_row=B)

    return _final_norm(xs, final_g, NL, tm=tm).reshape(B, SEQ, D)
```

```python
import functools

import jax
import jax.numpy as jnp
from jax import lax
from jax.experimental import pallas as pl
from jax.experimental.pallas import tpu as pltpu

F32 = jnp.float32
BF16 = jnp.bfloat16
I32 = jnp.int32

GRID_W = 64
N_HEADS = 16
Q_LORA = 1024
KV_LORA = 512
QK_NOPE = 128
QK_ROPE = 64
V_DIM = 128
ROPE_BASE = 10000.0
POOL_WINDOWS = (2, 4, 8, 16)
POOL_W = 1024
GM_CHUNK = 128
GM_GROUPS = 4
GM_W = 1024
N_EXPERTS = 16
CAPACITY_FACTOR = 2
EPS = 1e-6
LOG2_E = 1.4426950408889634

V7X_LANES = 128
V7X_VMEM_BYTES = 64 * 2**20
MXU_COLS = 256
MIB = 2**20

ROW_TILE = 512
PROJ_COLS = 1024
NORM_CHUNK = 16
POOL_HALO = 64
HEADS_PER_STEP = 4
ATTN_ROWS = 16
ATTN_KEYS = 512
ATTN_QUERIES = 512
ATTN_PAIRS_PER_TRIP = 2
MOE_TILE = 256
CAST_BLOCK_BYTES = 8 * MIB
ROUTE_BLOCK = 128
DMA_WORDS = 128


def _params(semantics, vmem_mib):
    return pltpu.CompilerParams(dimension_semantics=semantics, vmem_limit_bytes=int(vmem_mib * MIB))


def _pick(n, pref):
    best = None
    for t in range(V7X_LANES, min(n, pref) + 1, V7X_LANES):
        if n % t == 0:
            best = t
    assert best is not None, (n, pref)
    return best


def _ada_kernel(c_ref, w_ref, b_ref, o_ref):
    c = c_ref[...]
    s = c * jax.nn.sigmoid(c)
    acc = jnp.dot(s.astype(BF16), w_ref[...].astype(BF16), preferred_element_type=F32)
    o_ref[...] = acc + b_ref[...]


def _ada(cc, w_ada, b_ada):
    L, D, N = w_ada.shape
    R = cc.shape[0]
    tn = _pick(N, 512)
    return pl.pallas_call(
        _ada_kernel,
        out_shape=jax.ShapeDtypeStruct((L, R, N), F32),
        grid=(L, N // tn),
        in_specs=[
            pl.BlockSpec((R, D), lambda l, j: (0, 0)),
            pl.BlockSpec((None, D, tn), lambda l, j: (l, 0, j)),
            pl.BlockSpec((None, 1, tn), lambda l, j: (l, 0, j)),
        ],
        out_specs=pl.BlockSpec((None, R, tn), lambda l, j: (l, 0, j)),
        compiler_params=_params(("arbitrary", "arbitrary"), 2 * 2 * D * tn * 4 / MIB + 8),
        name="ada",
    )(cc, w_ada, b_ada.reshape(L, 1, N))


def _norm_rows(x_ref, a, b, dst_refs, rows):
    def body(c, carry):
        r = pl.multiple_of(c * NORM_CHUNK, NORM_CHUNK)
        x = x_ref[pl.ds(r, NORM_CHUNK), :].astype(F32)
        ms = jnp.mean(x * x, axis=-1, keepdims=True)
        h = x * lax.rsqrt(ms + EPS) * a + b
        for d in dst_refs:
            d[pl.ds(r, NORM_CHUNK), :] = h.astype(d.dtype)
        return carry

    lax.fori_loop(0, rows // NORM_CHUNK, body, 0, unroll=2)


def _affine(a_ref, b_ref, mode):
    a = a_ref[...]
    if mode == "mod":
        a = 1.0 + a
    return a, b_ref[...]


def _in_proj_kernel(x_ref, sc_ref, sh_ref, w_ref, pa_ref, g_ref, h_ref, *, n_plain):
    j = pl.program_id(1)

    @pl.when(j == 0)
    def _():
        a, b = _affine(sc_ref, sh_ref, "mod")
        _norm_rows(x_ref, a, b, (h_ref,), x_ref.shape[0])

    def chunks(emit):
        for c in range(w_ref.shape[1] // MXU_COLS):
            cols = slice(c * MXU_COLS, (c + 1) * MXU_COLS)
            emit(cols, jnp.dot(h_ref[...], w_ref[:, cols], preferred_element_type=F32))

    @pl.when(j < n_plain)
    def _():
        def emit(cols, acc):
            pa_ref[:, cols] = acc
        chunks(emit)

    @pl.when(j >= n_plain)
    def _():
        def emit(cols, acc):
            g_ref[:, cols] = jax.nn.sigmoid(acc).astype(g_ref.dtype)
        chunks(emit)


def _in_proj(x, sc, sh, w, *, n_plain, tm, tn, mod_fn):
    T, K = x.shape
    nj = w.shape[1] // tn
    vmem = (2 * tm * K * 4 + tm * K * 2 + 2 * K * tn * 2 + 2 * tm * tn * 6 + 2 * tm * tn * 4) / MIB + 6
    return pl.pallas_call(
        functools.partial(_in_proj_kernel, n_plain=n_plain),
        out_shape=(jax.ShapeDtypeStruct((T, n_plain * tn), F32),
                   jax.ShapeDtypeStruct((T, (nj - n_plain) * tn), BF16)),
        grid=(T // tm, nj),
        in_specs=[
            pl.BlockSpec((tm, K), lambda i, j: (i, 0)),
            pl.BlockSpec((None, 1, K), lambda i, j: (mod_fn(i), 0, 0)),
            pl.BlockSpec((None, 1, K), lambda i, j: (mod_fn(i), 0, 0)),
            pl.BlockSpec((K, tn), lambda i, j: (0, j)),
        ],
        out_specs=(pl.BlockSpec((tm, tn), lambda i, j: (i, jnp.minimum(j, n_plain - 1))),
                   pl.BlockSpec((tm, tn), lambda i, j: (i, jnp.maximum(j - n_plain, 0)))),
        scratch_shapes=[pltpu.VMEM((tm, K), BF16)],
        compiler_params=_params(("parallel", "arbitrary"), vmem),
        name="in_proj",
    )(x, sc, sh, w)


def _rope128(t, cos, sin):
    return t * cos + pltpu.roll(t, QK_ROPE, 1) * sin


def _q_kernel(x_ref, a_ref, b_ref, w_ref, cos_ref, sin_ref, o_ref, h_ref, *, scale):
    @pl.when(pl.program_id(1) == 0)
    def _():
        _norm_rows(x_ref, a_ref[...], b_ref[...], (h_ref,), x_ref.shape[0])

    acc = jnp.dot(h_ref[...], w_ref[...], preferred_element_type=F32)
    hq = o_ref.shape[2]
    for k in range(o_ref.shape[0]):
        rot = _rope128(acc[:, k * hq + QK_NOPE:(k + 1) * hq], cos_ref[...], sin_ref[...])
        o_ref[k, :, :QK_NOPE] = (acc[:, k * hq:k * hq + QK_NOPE] * scale).astype(o_ref.dtype)
        o_ref[k, :, QK_NOPE:] = (rot * scale).astype(o_ref.dtype)


def _kv_kernel(x_ref, a_ref, b_ref, w_ref, kr_ref, cos_ref, sin_ref, k_ref, v_ref, h_ref):
    @pl.when(pl.program_id(1) == 0)
    def _():
        _norm_rows(x_ref, a_ref[...], b_ref[...], (h_ref,), x_ref.shape[0])

    acc = jnp.dot(h_ref[...], w_ref[...], preferred_element_type=F32)
    rot = _rope128(kr_ref[...], cos_ref[...], sin_ref[...]).astype(k_ref.dtype)
    lane = lax.broadcasted_iota(I32, (v_ref.shape[1], V_DIM), 1)
    ones_col = jnp.where(lane == 0, 1.0, 0.0).astype(v_ref.dtype)
    hk = QK_NOPE + V_DIM
    for k in range(k_ref.shape[0]):
        k_ref[k, :, :QK_NOPE] = acc[:, k * hk:k * hk + QK_NOPE].astype(k_ref.dtype)
        k_ref[k, :, QK_NOPE:] = rot
        v_ref[k, :, :V_DIM] = acc[:, k * hk + QK_NOPE:(k + 1) * hk].astype(v_ref.dtype)
        v_ref[k, :, V_DIM:] = ones_col


def _lane_tile(x, width):
    return x if width == x.shape[1] else jnp.concatenate([x] * (width // x.shape[1]), axis=1)


def _attn_kernel(*refs, tk, n_lat):
    if n_lat:
        q_ref, kc_ref, vc_ref, kl_ref, vl_ref, o_ref, s_ref, p_ref, acc_ref, m_ref, al_ref = refs
    else:
        q_ref, kc_ref, vc_ref, o_ref, s_ref, p_ref, acc_ref, m_ref, al_ref = refs
    tq = q_ref.shape[0]
    ctx = kc_ref.shape[0]

    def scores(k):
        return lax.dot_general(q_ref[...], k, (((1,), (1,)), ((), ())), preferred_element_type=F32)

    def softmax(slot, width):
        for g in range(tq // ATTN_ROWS):
            rows = slice(g * ATTN_ROWS, (g + 1) * ATTN_ROWS)
            sg = s_ref[slot, rows, :width]
            m_old = m_ref[rows, :]
            m_new = jnp.maximum(m_old, jnp.max(sg, axis=-1, keepdims=True))
            p_ref[slot, rows, :width] = jnp.exp2(sg - _lane_tile(m_new, width)).astype(BF16)
            al_ref[slot, rows, :] = jnp.exp2(m_old - m_new)
            m_ref[rows, :] = m_new

    def accumulate(slot, width, v):
        acc_ref[...] = (_lane_tile(al_ref[slot], acc_ref.shape[1]) * acc_ref[...]
                        + jnp.dot(p_ref[slot, :, :width], v, preferred_element_type=F32))

    def lat(ref, c):
        if isinstance(c, int):
            return ref[c * tk:(c + 1) * tk, :]
        return ref[pl.ds(pl.multiple_of(c * tk, tk), tk), :]

    m_ref[...] = jnp.full(m_ref.shape, -jnp.inf, F32)
    acc_ref[...] = jnp.zeros(acc_ref.shape, F32)
    s_ref[0, :, :ctx] = scores(kc_ref[...])
    if n_lat:
        s_ref[1] = scores(kl_ref[0:tk, :])
    softmax(0, ctx)
    if n_lat:
        def step(c, odd, last):
            if not last:
                nxt = min(c + 1, n_lat - 1) if isinstance(c, int) else jnp.minimum(c + 1, n_lat - 1)
                s_ref[1 if odd else 0] = scores(lat(kl_ref, nxt))
            softmax(0 if odd else 1, tk)
            accumulate(1 if odd else 0, tk, lat(vl_ref, c - 1))

        if n_lat > 1:
            s_ref[0] = scores(kl_ref[tk:2 * tk, :])
        softmax(1, tk)
        accumulate(0, ctx, vc_ref[...])
        pairs = (n_lat - 1) // 2

        def body(j, carry):
            step(1 + 2 * j, True, False)
            step(2 + 2 * j, False, False)
            return carry

        lax.fori_loop(0, pairs, body, 0, unroll=min(ATTN_PAIRS_PER_TRIP, max(pairs, 1)))
        if (n_lat - 1) % 2:
            step(n_lat - 1, True, True)
        accumulate(n_lat % 2, tk, vl_ref[(n_lat - 1) * tk:n_lat * tk, :])
    else:
        accumulate(0, ctx, vc_ref[...])
    acc = acc_ref[...]
    o_ref[...] = (acc[:, :V_DIM] / acc[:, V_DIM:V_DIM + 1]).astype(o_ref.dtype)


def _attn_scratch(tq, tk, dv):
    return [pltpu.VMEM((2, tq, tk), F32), pltpu.VMEM((2, tq, tk), BF16), pltpu.VMEM((tq, dv), F32),
            pltpu.VMEM((tq, V7X_LANES), F32), pltpu.VMEM((2, tq, V7X_LANES), F32)]


def _attention(qc, kc, vc, *, B, SEQ, CTX):
    H, T, dq = qc.shape
    dv = vc.shape[2]
    NL = B * SEQ
    tq = min(ATTN_QUERIES, SEQ)
    tk = max(min(ATTN_KEYS, SEQ), CTX)
    assert SEQ % tk == 0 and SEQ % tq == 0 and tk >= CTX and tq % ATTN_ROWS == 0 and CTX % ATTN_ROWS == 0
    nq = SEQ // tq
    cblk = NL // CTX
    lat = pl.pallas_call(
        functools.partial(_attn_kernel, tk=tk, n_lat=SEQ // tk),
        out_shape=jax.ShapeDtypeStruct((NL, H * V_DIM), BF16),
        grid=(B, H, nq),
        in_specs=[
            pl.BlockSpec((None, tq, dq), lambda b, h, i: (h, b * nq + i, 0)),
            pl.BlockSpec((None, CTX, dq), lambda b, h, i: (h, cblk + b, 0)),
            pl.BlockSpec((None, CTX, dv), lambda b, h, i: (h, cblk + b, 0)),
            pl.BlockSpec((None, SEQ, dq), lambda b, h, i: (h, b, 0)),
            pl.BlockSpec((None, SEQ, dv), lambda b, h, i: (h, b, 0)),
        ],
        out_specs=pl.BlockSpec((tq, V_DIM), lambda b, h, i: (b * nq + i, h)),
        scratch_shapes=_attn_scratch(tq, tk, dv),
        compiler_params=_params(("parallel", "parallel", "arbitrary"),
                                2 * SEQ * (dq + dv) * 2 / MIB + 24),
        name="attn_latent",
    )(qc, kc, vc, kc, vc)
    ctx = pl.pallas_call(
        functools.partial(_attn_kernel, tk=tk, n_lat=0),
        out_shape=jax.ShapeDtypeStruct((B * CTX, H * V_DIM), BF16),
        grid=(B, H),
        in_specs=[
            pl.BlockSpec((None, CTX, dq), lambda b, h: (h, cblk + b, 0)),
            pl.BlockSpec((None, CTX, dq), lambda b, h: (h, cblk + b, 0)),
            pl.BlockSpec((None, CTX, dv), lambda b, h: (h, cblk + b, 0)),
        ],
        out_specs=pl.BlockSpec((CTX, V_DIM), lambda b, h: (b, h)),
        scratch_shapes=_attn_scratch(CTX, CTX, dv),
        compiler_params=_params(("parallel", "arbitrary"), 16),
        name="attn_ctx",
    )(qc, kc, vc)
    return jnp.concatenate([lat, ctx], axis=0)


def _pool_kernel(prev_ref, x_ref, next_ref, w_ref, ps_ref, o_ref, *, NL, SEQ, CTX):
    tm, width = x_ref.shape
    gw = width // len(POOL_WINDOWS)
    row0 = pl.program_id(0) * tm
    xm = x_ref[...]
    xcat = jnp.concatenate([prev_ref[...], xm, next_ref[...]], axis=0)
    hi_part = xcat.astype(BF16)
    lo_part = (xcat - hi_part.astype(F32)).astype(BF16)

    r = row0 + lax.broadcasted_iota(I32, (tm, 1), 0)
    s = row0 - POOL_HALO + lax.broadcasted_iota(I32, (1, tm + 2 * POOL_HALO), 1)
    lat_lo = (row0 // SEQ) * SEQ
    ctx_lo = jnp.full((tm, 1), row0, I32)
    for k in range(1, max(tm // CTX, 1)):
        ctx_lo = jnp.where(r >= row0 + k * CTX, row0 + k * CTX, ctx_lo)
    is_lat = row0 < NL
    seq_lo = jnp.where(is_lat, lat_lo, ctx_lo)
    seq_hi = seq_lo + jnp.where(is_lat, SEQ, CTX)

    for g, win in enumerate(POOL_WINDOWS):
        left = win // 2
        right = win - 1 - left
        lo = jnp.maximum(r - left, seq_lo)
        hi = jnp.minimum(r + right + 1, seq_hi)
        band = jnp.where((s >= lo) & (s < hi), 1.0, 0.0).astype(BF16)
        cols = slice(g * gw, (g + 1) * gw)
        tot = (jnp.dot(band, hi_part[:, cols], preferred_element_type=F32)
               + jnp.dot(band, lo_part[:, cols], preferred_element_type=F32))
        d = tot / (hi - lo).astype(F32) - xm[:, cols]
        y = jnp.dot(d.astype(BF16), w_ref[g], preferred_element_type=F32)
        o_ref[:, cols] = (y * ps_ref[:, cols]).astype(o_ref.dtype)


def _pool(pa, col, w_pool, pool_scale, *, tm, NL, SEQ, CTX):
    T = pa.shape[0]
    hb = tm // POOL_HALO
    last = T // POOL_HALO - 1
    return pl.pallas_call(
        functools.partial(_pool_kernel, NL=NL, SEQ=SEQ, CTX=CTX),
        out_shape=jax.ShapeDtypeStruct((T, POOL_W), BF16),
        grid=(T // tm,),
        in_specs=[
            pl.BlockSpec((POOL_HALO, POOL_W), lambda i: (jnp.maximum(i * hb - 1, 0), col)),
            pl.BlockSpec((tm, POOL_W), lambda i: (i, col)),
            pl.BlockSpec((POOL_HALO, POOL_W), lambda i: (jnp.minimum((i + 1) * hb, last), col)),
            pl.BlockSpec(w_pool.shape, lambda i: (0, 0, 0)),
            pl.BlockSpec((1, POOL_W), lambda i: (0, 0)),
        ],
        out_specs=pl.BlockSpec((tm, POOL_W), lambda i: (i, 0)),
        compiler_params=_params(("parallel",), 40),
        name="pool",
    )(pa, pa, pa, w_pool, pool_scale)


def _gelu(z):
    return 0.5 * z * (1.0 + lax.erf(z * (2.0 ** -0.5)))


def _gm_kernel(u_ref, v_ref, ws_ref, bias_ref, o_ref):
    tm = u_ref.shape[0]
    gw = GM_W // GM_GROUPS

    def body(c, carry):
        r = pl.multiple_of(c * GM_CHUNK, GM_CHUNK)
        zu = _gelu(u_ref[pl.ds(r, GM_CHUNK), :])
        zv = _gelu(v_ref[pl.ds(r, GM_CHUNK), :])
        mu = jnp.mean(zv, axis=-1, keepdims=True)
        var = jnp.mean(jnp.square(zv - mu), axis=-1, keepdims=True)
        vn = ((zv - mu) * lax.rsqrt(var + EPS)).astype(BF16)
        for g in range(GM_GROUPS):
            cols = slice(g * gw, (g + 1) * gw)
            sv = jnp.dot(ws_ref[g], vn[:, cols], preferred_element_type=F32) + bias_ref[:, cols]
            o_ref[pl.ds(r, GM_CHUNK), cols] = (zu[:, cols] * sv).astype(o_ref.dtype)
        return carry

    lax.fori_loop(0, tm // GM_CHUNK, body, 0)


def _gm(pa, ucol, w_s, bias_full, *, tm):
    T = pa.shape[0]
    return pl.pallas_call(
        _gm_kernel,
        out_shape=jax.ShapeDtypeStruct((T, GM_W), BF16),
        grid=(T // tm,),
        in_specs=[
            pl.BlockSpec((tm, GM_W), lambda i: (i, ucol)),
            pl.BlockSpec((tm, GM_W), lambda i: (i, ucol + 1)),
            pl.BlockSpec(w_s.shape, lambda i: (0, 0, 0)),
            pl.BlockSpec(bias_full.shape, lambda i: (0, 0)),
        ],
        out_specs=pl.BlockSpec((tm, GM_W), lambda i: (i, 0)),
        compiler_params=_params(("parallel",), 32),
        name="gmlp",
    )(pa, pa, w_s, bias_full)


def _branch_kernel(at_ref, po_ref, gm_ref, w_ref, g0_ref, g1_ref, g2_ref, o_ref):
    ka = at_ref.shape[1]
    kp = po_ref.shape[1]
    y = g0_ref[...].astype(F32) * jnp.dot(at_ref[...], w_ref[0:ka, :], preferred_element_type=F32)
    y += g1_ref[...].astype(F32) * jnp.dot(po_ref[...], w_ref[ka:ka + kp, :], preferred_element_type=F32)
    y += g2_ref[...].astype(F32) * jnp.dot(gm_ref[...], w_ref[ka + kp:, :], preferred_element_type=F32)
    o_ref[...] = y.astype(o_ref.dtype)


def _branch(attn, pool, gm, w_branch, gates, *, tm, tn):
    T = attn.shape[0]
    K, D = w_branch.shape
    nj = D // tn
    return pl.pallas_call(
        _branch_kernel,
        out_shape=jax.ShapeDtypeStruct((T, D), BF16),
        grid=(T // tm, nj),
        in_specs=[
            pl.BlockSpec((tm, attn.shape[1]), lambda i, j: (i, 0)),
            pl.BlockSpec((tm, pool.shape[1]), lambda i, j: (i, 0)),
            pl.BlockSpec((tm, gm.shape[1]), lambda i, j: (i, 0)),
            pl.BlockSpec((K, tn), lambda i, j: (0, j)),
            pl.BlockSpec((tm, tn), lambda i, j: (i, j)),
            pl.BlockSpec((tm, tn), lambda i, j: (i, nj + j)),
            pl.BlockSpec((tm, tn), lambda i, j: (i, 2 * nj + j)),
        ],
        out_specs=pl.BlockSpec((tm, tn), lambda i, j: (i, j)),
        compiler_params=_params(("parallel", "arbitrary"),
                                (2 * tm * K * 2 + 2 * K * tn * 2 + 8 * tm * tn * 2 + 4 * tm * tn * 4) / MIB + 8),
        name="branch_mix",
    )(attn, pool, gm, w_branch, gates, gates, gates)


def _mm_res_kernel(a_ref, w_ref, r_ref, g_ref, o_ref):
    acc = jnp.dot(a_ref[...], w_ref[...], preferred_element_type=F32)
    o_ref[...] = r_ref[...] + g_ref[...] * acc


def _mm_res(a, w, res, gate, *, tm, tn, mod_fn):
    T, K = a.shape
    N = w.shape[1]
    return pl.pallas_call(
        _mm_res_kernel,
        out_shape=jax.ShapeDtypeStruct((T, N), F32),
        grid=(T // tm, N // tn),
        in_specs=[
            pl.BlockSpec((tm, K), lambda i, j: (i, 0)),
            pl.BlockSpec((K, tn), lambda i, j: (0, j)),
            pl.BlockSpec((tm, tn), lambda i, j: (i, j)),
            pl.BlockSpec((None, 1, tn), lambda i, j: (mod_fn(i), 0, j)),
        ],
        out_specs=pl.BlockSpec((tm, tn), lambda i, j: (i, j)),
        input_output_aliases={2: 0},
        compiler_params=_params(("parallel", "arbitrary"),
                                (2 * tm * K * 2 + 2 * K * tn * 2 + 6 * tm * tn * 4) / MIB + 8),
        name="out_proj",
    )(a, w, res, gate)


def _mod_router_kernel(x_ref, sc_ref, sh_ref, wr_ref, h_ref, lg_ref, hb_ref):
    a, b = _affine(sc_ref, sh_ref, "mod")
    _norm_rows(x_ref, a, b, (h_ref, hb_ref), x_ref.shape[0])
    lg_ref[...] = lax.dot_general(wr_ref[...], hb_ref[...], (((1,), (1,)), ((), ())),
                                  preferred_element_type=F32)


def _mod_router(xs, sc, sh, wr_t, *, tm, mod_fn):
    T, D = xs.shape
    E = wr_t.shape[0]
    return pl.pallas_call(
        _mod_router_kernel,
        out_shape=(jax.ShapeDtypeStruct((T, D), F32), jax.ShapeDtypeStruct((E, T), F32)),
        grid=(T // tm,),
        in_specs=[
            pl.BlockSpec((tm, D), lambda i: (i, 0)),
            pl.BlockSpec((None, 1, D), lambda i: (mod_fn(i), 0, 0)),
            pl.BlockSpec((None, 1, D), lambda i: (mod_fn(i), 0, 0)),
            pl.BlockSpec((E, D), lambda i: (0, 0)),
        ],
        out_specs=(pl.BlockSpec((tm, D), lambda i: (i, 0)), pl.BlockSpec((E, tm), lambda i: (0, i))),
        scratch_shapes=[pltpu.VMEM((tm, D), BF16)],
        compiler_params=_params(("parallel",), (4 * tm * D * 4 + tm * D * 2) / MIB + 8),
        name="mod_router",
    )(xs, sc, sh, wr_t)


def _route_kernel(lg_ref, li_ref, lg_out_ref, cnt_ref, code_ref, *, cap):
    lg = lg_ref[...]
    E, n = lg.shape
    mx = jnp.max(lg, axis=0, keepdims=True)
    ex = jnp.exp(lg - mx)
    aff = ex / jnp.sum(ex, axis=0, keepdims=True)
    bits = pltpu.bitcast(aff, I32)

    def count(mask):
        return jnp.sum(jnp.where(mask, 1.0, 0.0), axis=1, keepdims=True)

    v = jnp.zeros((E, 1), I32)
    for bit in range(30, -1, -1):
        cand = v | (1 << bit)
        v = jnp.where(count(bits >= cand) >= cap, cand, v)
    gt = bits > v
    eq = bits == v
    need = cap - count(gt)
    t = lax.broadcasted_iota(I32, (E, n), 1)
    u = jnp.zeros((E, 1), I32)
    for bit in range(max(n - 1, 1).bit_length() - 1, -1, -1):
        cand = u | (1 << bit)
        u = jnp.where(count(eq & (t < cand)) < need, cand, u)
    sel = gt | (eq & (t <= u))
    code_ref[...] = jnp.where(sel, bits | jnp.int32(-2**31), bits)

    W = ROUTE_BLOCK
    sub = lax.broadcasted_iota(I32, (W, W), 0)
    lane_sq = lax.broadcasted_iota(I32, (W, W), 1)
    below = jnp.where(sub < lane_sq, 1.0, 0.0).astype(BF16)
    slot = sub.astype(F32)
    lane = lax.broadcasted_iota(I32, (E, W), 1)
    head = jnp.where(lax.broadcasted_iota(I32, (E, W), 0) == 0, lane.astype(F32), 0.0).astype(BF16)

    def block(k, cnt_acc):
        off = pl.multiple_of(k * W, W)
        w = code_ref[:, pl.ds(off, W)]
        picked = w < 0
        a = pltpu.bitcast(w & 0x7FFFFFFF, F32)
        ones = jnp.where(picked, 1.0, 0.0)
        pos = jnp.dot(ones.astype(BF16), below, preferred_element_type=F32)
        pos = jnp.where(picked, pos, -1.0)
        hi = a.astype(BF16)
        r1 = a - hi.astype(F32)
        mid = r1.astype(BF16)
        lo = (r1 - mid.astype(F32)).astype(BF16)
        vals = jnp.concatenate([head, hi, mid, lo], axis=0)
        for e in range(E):
            onehot = jnp.where(slot == jnp.broadcast_to(pos[e:e + 1, :], (W, W)), 1.0, 0.0).astype(BF16)
            c = lax.dot_general(vals, onehot, (((1,), (1,)), ((), ())), preferred_element_type=F32)
            li_ref[e:e + 1, pl.ds(off, W)] = c[0:1, :].astype(I32) + k * W
            lg_out_ref[e:e + 1, pl.ds(off, W)] = (c[E + e:E + e + 1, :] + c[2 * E + e:2 * E + e + 1, :]
                                                   + c[3 * E + e:3 * E + e + 1, :])
        return cnt_acc + jnp.where(lane == k, jnp.sum(ones, axis=1, keepdims=True), 0.0)

    cnt = lax.fori_loop(0, n // W, block, jnp.zeros((E, W), F32))
    cnt_ref[...] = cnt.astype(I32)


def _route(logits_t, *, n, cap, blk0, sets):
    E = logits_t.shape[0]
    assert n % ROUTE_BLOCK == 0 and n // ROUTE_BLOCK <= ROUTE_BLOCK and E % 16 == 0
    return pl.pallas_call(
        functools.partial(_route_kernel, cap=cap),
        out_shape=(jax.ShapeDtypeStruct((sets, E, n), I32), jax.ShapeDtypeStruct((sets, E, n), F32),
                   jax.ShapeDtypeStruct((sets, E, ROUTE_BLOCK), I32)),
        grid=(sets,),
        in_specs=[pl.BlockSpec((E, n), lambda s: (0, blk0 + s))],
        out_specs=(pl.BlockSpec((None, E, n), lambda s: (s, 0, 0)), pl.BlockSpec((None, E, n), lambda s: (s, 0, 0)),
                   pl.BlockSpec((None, E, ROUTE_BLOCK), lambda s: (s, 0, 0))),
        scratch_shapes=[pltpu.VMEM((E, n), I32)],
        compiler_params=_params(("parallel",), 32),
        name="route_select",
    )(logits_t)


def _compact_kernel(li_hbm, lg_hbm, cnt_hbm, idx_hbm, gate_hbm, li_s, lg_s, cnt_s, idx_s, gate_s, sem_in, sem_out,
                    *, n, cap):
    s = pl.program_id(0)
    e = pl.program_id(1)
    loads = [pltpu.make_async_copy(li_hbm.at[s, e], li_s, sem_in.at[0]),
             pltpu.make_async_copy(lg_hbm.at[s, e], lg_s, sem_in.at[1]),
             pltpu.make_async_copy(cnt_hbm.at[s, e], cnt_s, sem_in.at[2])]
    for cp in loads:
        cp.start()
    for cp in loads:
        cp.wait()

    def block(k, c0):
        nk = cnt_s[k]

        def entry(j, carry):
            idx_s[c0 + j] = li_s[k * ROUTE_BLOCK + j]
            gate_s[c0 + j] = lg_s[k * ROUTE_BLOCK + j]
            return carry

        lax.fori_loop(0, nk, entry, 0)
        return c0 + nk

    lax.fori_loop(0, n // ROUTE_BLOCK, block, jnp.int32(0))
    cap_pad = idx_hbm.shape[-1]

    def clear(p, carry):
        idx_s[p] = 0
        gate_s[p] = 0.0
        return carry

    lax.fori_loop(cap, cap_pad, clear, 0)
    stores = [pltpu.make_async_copy(idx_s, idx_hbm.at[s, e], sem_out.at[0]),
              pltpu.make_async_copy(gate_s, gate_hbm.at[s, e], sem_out.at[1])]
    for cp in stores:
        cp.start()
    for cp in stores:
        cp.wait()


def _compact(li, lg, cnt, *, cap):
    sets, E, n = li.shape
    cap_pad = -(-cap // DMA_WORDS) * DMA_WORDS
    idx, gates = pl.pallas_call(
        functools.partial(_compact_kernel, n=n, cap=cap),
        out_shape=(jax.ShapeDtypeStruct((sets, E, cap_pad), I32), jax.ShapeDtypeStruct((sets, E, cap_pad), F32)),
        grid=(sets, E),
        in_specs=[pl.BlockSpec(memory_space=pl.ANY)] * 3,
        out_specs=(pl.BlockSpec(memory_space=pl.ANY), pl.BlockSpec(memory_space=pl.ANY)),
        scratch_shapes=[pltpu.SMEM((n,), I32), pltpu.SMEM((n,), F32), pltpu.SMEM((ROUTE_BLOCK,), I32),
                        pltpu.SMEM((cap_pad,), I32), pltpu.SMEM((cap_pad,), F32),
                        pltpu.SemaphoreType.DMA((3,)), pltpu.SemaphoreType.DMA((2,))],
        compiler_params=_params(("arbitrary", "arbitrary"), 16),
        name="route_compact",
    )(li, lg, cnt)
    return idx[:, :, :cap], gates[:, :, :cap]


def _row_copy(hbm, vmem, slot, r, row, sem, to_hbm):
    h, v = hbm.at[pl.ds(row, 1)], vmem.at[slot, pl.ds(r, 1)]
    return pltpu.make_async_copy(v, h, sem.at[slot]) if to_hbm else pltpu.make_async_copy(h, v, sem.at[slot])


def _row_starts(hbm, vmem, idx_s, slot, sem, rows, *, to_hbm=False):
    for r in range(rows):
        _row_copy(hbm, vmem, slot, r, idx_s[slot, r], sem, to_hbm).start()


def _row_waits(hbm, vmem, slot, sem, rows, *, to_hbm=False):
    def body(r, carry):
        _row_copy(hbm, vmem, slot, r, 0, sem, to_hbm).wait()
        return carry

    lax.fori_loop(0, rows, body, 0, unroll=8)


def _per_slot(slot, fn, when=True):
    for sl in (0, 1):
        pl.when(jnp.logical_and(slot == sl, when))(functools.partial(fn, sl))


def _idx_copy(idx_hbm, idx_s, slot, sem, step):
    return pltpu.make_async_copy(idx_hbm.at[step], idx_s.at[slot], sem)


def _step_ids():
    step = pl.program_id(0) * pl.num_programs(1) + pl.program_id(1)
    return step, pl.num_programs(0) * pl.num_programs(1), lax.rem(step, 2)


def _ffn1_kernel(idx_hbm, h_hbm, wg_ref, wu_ref, o_ref, xbuf, idx_s, sem_i, sem_g):
    tm = xbuf.shape[1]
    step, total, slot = _step_ids()

    def fetch(st, sl):
        cp = _idx_copy(idx_hbm, idx_s, sl, sem_i, st)
        cp.start()
        cp.wait()
        _row_starts(h_hbm, xbuf, idx_s, sl, sem_g, tm)

    @pl.when(step == 0)
    def _():
        fetch(0, 0)

    _per_slot(slot, lambda sl: fetch(step + 1, 1 - sl), when=step + 1 < total)
    _row_waits(h_hbm, xbuf, slot, sem_g, tm)
    xb = xbuf[slot].astype(BF16)
    a = jnp.dot(xb, wg_ref[...], preferred_element_type=F32)
    u = jnp.dot(xb, wu_ref[...], preferred_element_type=F32)
    o_ref[...] = (a * jax.nn.sigmoid(a) * u).astype(o_ref.dtype)


def _ffn1(idx_tiles, h2, w_gate, w_up, *, tm):
    E, D, F = w_gate.shape
    tiles = idx_tiles.shape[0] // E
    return pl.pallas_call(
        _ffn1_kernel,
        out_shape=jax.ShapeDtypeStruct((E, tiles * tm, F), BF16),
        grid=(E, tiles),
        in_specs=[
            pl.BlockSpec(memory_space=pl.ANY),
            pl.BlockSpec(memory_space=pl.ANY),
            pl.BlockSpec((None, D, F), lambda e, m: (e, 0, 0)),
            pl.BlockSpec((None, D, F), lambda e, m: (e, 0, 0)),
        ],
        out_specs=pl.BlockSpec((None, tm, F), lambda e, m: (e, m, 0)),
        scratch_shapes=[pltpu.VMEM((2, tm, D), F32), pltpu.SMEM((2, max(tm, DMA_WORDS)), I32),
                        pltpu.SemaphoreType.DMA(()), pltpu.SemaphoreType.DMA((2,))],
        compiler_params=_params(("arbitrary", "arbitrary"),
                                (4 * D * F * 2 + tm * D * 10 + 4 * tm * F * 4) / MIB + 6),
        name="expert_ffn_in",
    )(idx_tiles, h2, w_gate, w_up)


def _ffn2_kernel(idx_hbm, hh_ref, wd_ref, gate_ref, g2_ref, acc_in, o_hbm, abuf, idx_s, sem_i, sem_g, sem_s,
                 *, overlap):
    del acc_in
    tm = abuf.shape[1]
    step, total, slot = _step_ids()

    def fetch(st, sl):
        cp = _idx_copy(idx_hbm, idx_s, sl, sem_i, st)
        cp.start()
        cp.wait()
        _row_starts(o_hbm, abuf, idx_s, sl, sem_g, tm)

    if overlap:
        @pl.when(step == 0)
        def _():
            fetch(0, 0)
    else:
        fetch(step, slot)

    y = jnp.dot(hh_ref[...], wd_ref[...], preferred_element_type=F32)
    y = y * gate_ref[...] * g2_ref[...]
    _row_waits(o_hbm, abuf, slot, sem_g, tm)
    abuf[slot] = abuf[slot] + y
    _per_slot(slot, lambda sl: _row_starts(o_hbm, abuf, idx_s, sl, sem_s, tm, to_hbm=True))

    if overlap:
        @pl.when(step > 0)
        def _():
            _row_waits(o_hbm, abuf, 1 - slot, sem_s, tm, to_hbm=True)

        _per_slot(slot, lambda sl: fetch(step + 1, 1 - sl), when=step + 1 < total)

        @pl.when(step + 1 == total)
        def _():
            _row_waits(o_hbm, abuf, slot, sem_s, tm, to_hbm=True)
    else:
        _row_waits(o_hbm, abuf, slot, sem_s, tm, to_hbm=True)


def _ffn2_scatter(idx_tiles, hh, w_down, gates, g2, xs, *, tm, mod_fn, overlap):
    E, F, D = w_down.shape
    tiles = idx_tiles.shape[0] // E
    return pl.pallas_call(
        functools.partial(_ffn2_kernel, overlap=overlap),
        out_shape=jax.ShapeDtypeStruct(xs.shape, xs.dtype),
        grid=(E, tiles),
        in_specs=[
            pl.BlockSpec(memory_space=pl.ANY),
            pl.BlockSpec((None, tm, F), lambda e, m: (e, m, 0)),
            pl.BlockSpec((None, F, D), lambda e, m: (e, 0, 0)),
            pl.BlockSpec((None, tm, 1), lambda e, m: (e, m, 0)),
            pl.BlockSpec((None, 1, D), lambda e, m: (mod_fn(m), 0, 0)),
            pl.BlockSpec(memory_space=pl.ANY),
        ],
        out_specs=pl.BlockSpec(memory_space=pl.ANY),
        scratch_shapes=[pltpu.VMEM((2, tm, D), F32), pltpu.SMEM((2, max(tm, DMA_WORDS)), I32),
                        pltpu.SemaphoreType.DMA(()), pltpu.SemaphoreType.DMA((2,)), pltpu.SemaphoreType.DMA((2,))],
        input_output_aliases={5: 0},
        compiler_params=_params(("arbitrary", "arbitrary"),
                                (2 * F * D * 2 + 4 * tm * D * 4 + 4 * tm * F * 2) / MIB + 8),
        name="expert_ffn_out",
    )(idx_tiles, hh, w_down, gates, g2, xs)


def _moe(xs, logits_t, h2, g2, w_gate, w_up, w_down, *, n, sets, row0, mod_row):
    E = logits_t.shape[0]
    cap = (CAPACITY_FACTOR * n) // E
    idx, aff = _compact(*_route(logits_t, n=n, cap=cap, blk0=row0 // n, sets=sets), cap=cap)
    tm = min(MOE_TILE, cap if mod_row is None else sets * cap)
    assert (sets * cap) % tm == 0 and (mod_row is not None or cap % tm == 0)
    rows = idx + (row0 + n * jnp.arange(sets, dtype=I32))[:, None, None]
    idx_tiles = jnp.transpose(rows, (1, 0, 2)).reshape(E * sets * cap // tm, tm)
    idx_tiles = jnp.pad(idx_tiles, ((0, 0), (0, max(DMA_WORDS - tm, 0))))
    gates = jnp.transpose(aff, (1, 0, 2)).reshape(E, sets * cap, 1)
    hh = _ffn1(idx_tiles, h2, w_gate, w_up, tm=tm)
    mod_fn = (lambda m: m // (cap // tm)) if mod_row is None else (lambda m: mod_row)
    overlap = mod_row is None and sets >= 2
    return _ffn2_scatter(idx_tiles, hh, w_down, gates, g2, xs, tm=tm, mod_fn=mod_fn, overlap=overlap)


def _cast_kernel(x_ref, o_ref):
    o_ref[...] = x_ref[...].astype(o_ref.dtype)


def _layer_bf16(w, l):
    L, shape = w.shape[0], w.shape[1:]
    C = shape[-1]
    R = 1
    for d in shape[:-1]:
        R *= d
    br = R
    while br * C * 4 > CAST_BLOCK_BYTES and br % 16 == 0:
        br //= 2
    nb = R // br
    out = pl.pallas_call(
        _cast_kernel,
        out_shape=jax.ShapeDtypeStruct((R, C), BF16),
        grid=(nb,),
        in_specs=[pl.BlockSpec((br, C), lambda i: (l * nb + i, 0))],
        out_specs=pl.BlockSpec((br, C), lambda i: (i, 0)),
        compiler_params=_params(("parallel",), 2 * br * C * 6 / MIB + 4),
        name="weight_cast",
    )(w.reshape(L * R, C))
    return out.reshape(shape)


def _final_norm_kernel(x_ref, g_ref, z_ref, o_ref):
    _norm_rows(x_ref, g_ref[...], z_ref[...], (o_ref,), x_ref.shape[0])


def _final_norm(xs, gain, rows, *, tm):
    D = xs.shape[1]
    return pl.pallas_call(
        _final_norm_kernel,
        out_shape=jax.ShapeDtypeStruct((rows, D), F32),
        grid=(rows // tm,),
        in_specs=[
            pl.BlockSpec((tm, D), lambda i: (i, 0)),
            pl.BlockSpec((1, D), lambda i: (0, 0)),
            pl.BlockSpec((1, D), lambda i: (0, 0)),
        ],
        out_specs=pl.BlockSpec((tm, D), lambda i: (i, 0)),
        compiler_params=_params(("parallel",), 4 * tm * D * 4 / MIB + 8),
        name="final_norm",
    )(xs, gain.reshape(1, D), jnp.zeros((1, D), F32))


def _rope_tables(B, SEQ, CTX):
    rows = SEQ // GRID_W
    row = jnp.broadcast_to(jnp.arange(rows, dtype=F32)[:, None], (rows, GRID_W)).reshape(-1)
    col = jnp.broadcast_to(jnp.arange(GRID_W, dtype=F32)[None, :], (rows, GRID_W)).reshape(-1)
    n_freq = QK_ROPE // 4
    inv = ROPE_BASE ** (-jnp.arange(n_freq, dtype=F32) / n_freq)
    ar = row[:, None] * inv
    ac = col[:, None] * inv
    cos = jnp.concatenate([jnp.cos(ar), jnp.cos(ar), jnp.cos(ac), jnp.cos(ac)], axis=-1)
    sin = jnp.concatenate([-jnp.sin(ar), jnp.sin(ar), -jnp.sin(ac), jnp.sin(ac)], axis=-1)
    pad = jnp.zeros((SEQ, 2 * QK_ROPE - QK_ROPE), F32)
    cos_l = jnp.concatenate([cos, pad], axis=-1)
    sin_l = jnp.concatenate([sin, pad], axis=-1)
    cos_c = jnp.concatenate([jnp.ones((B * CTX, QK_ROPE), F32), jnp.zeros((B * CTX, QK_ROPE), F32)], axis=-1)
    cos_t = jnp.concatenate([jnp.tile(cos_l, (B, 1)), cos_c], axis=0)
    sin_t = jnp.concatenate([jnp.tile(sin_l, (B, 1)), jnp.zeros((B * CTX, 2 * QK_ROPE), F32)], axis=0)
    return cos_t, sin_t


def _swap_perm():
    q = QK_ROPE // 4
    return jnp.concatenate([jnp.arange(q, 2 * q), jnp.arange(0, q), jnp.arange(3 * q, 4 * q), jnp.arange(2 * q, 3 * q)])


def kernel(x, c, ctx, c_ctx, w_ada, b_ada, w_in, q_norm_g, w_uq, kv_norm_g, w_ukv, w_pool, pool_scale,
           w_spatial, b_spatial, w_branch, w_out, w_router, w_gate, w_up, w_down, final_g):
    B, SEQ, D = x.shape
    CTX = ctx.shape[1]
    L = w_ada.shape[0]
    H = N_HEADS
    NL, NC = B * SEQ, B * CTX
    T = NL + NC
    tm = min(ROW_TILE, NC)
    assert SEQ % tm == 0 and NC % tm == 0 and tm % CTX == 0 and CTX % GM_CHUNK == 0 and SEQ % GM_CHUNK == 0
    assert QK_NOPE == V7X_LANES and 2 * QK_ROPE == V7X_LANES and V_DIM == V7X_LANES
    assert Q_LORA == POOL_W == GM_W and Q_LORA % KV_LORA == 0 and KV_LORA % V7X_LANES == 0
    tiles_per_seq = SEQ // tm

    def mod_tile(i):
        return jnp.minimum(i // tiles_per_seq, B)

    kv_start = Q_LORA
    kr_start = kv_start + KV_LORA
    pool_start = kr_start + QK_ROPE
    gm_start = pool_start + POOL_W
    gate_start = gm_start + 2 * GM_W
    wa_cols = Q_LORA + POOL_W + 2 * GM_W + KV_LORA + 2 * QK_ROPE
    tn_in = _pick(3 * D, PROJ_COLS)
    wa_pad = -wa_cols % tn_in
    perm = _swap_perm()
    dh = QK_NOPE + QK_ROPE
    scale = float(dh) ** -0.5 * LOG2_E

    cos_t, sin_t = _rope_tables(B, SEQ, CTX)
    xs = jnp.concatenate([x.reshape(NL, D), ctx.reshape(NC, D)], axis=0)

    cc = jnp.zeros((8, D), F32).at[:B].set(c).at[B].set(c_ctx)
    mods = _ada(cc, w_ada, b_ada)[:, :B + 1].reshape(L, B + 1, 6, 1, D)

    zeros_q = jnp.zeros((1, 1, Q_LORA), F32)
    zeros_kv = jnp.zeros((1, 1, KV_LORA), F32)

    for l in range(L):
        sh1, sc1, g1, sh2, sc2, g2 = (mods[l, :, k] for k in range(6))
        wl = w_in[l]
        kr_w = wl[:, kr_start:kr_start + QK_ROPE]
        w_all = jnp.concatenate(
            [wl[:, :Q_LORA], wl[:, pool_start:pool_start + POOL_W], wl[:, gm_start:gm_start + 2 * GM_W],
             wl[:, kv_start:kv_start + KV_LORA], kr_w, kr_w[:, perm], jnp.zeros((D, wa_pad), F32),
             wl[:, gate_start:]], axis=1).astype(BF16)
        uq = w_uq[l].reshape(Q_LORA, H, dh)
        uq_ext = jnp.concatenate([uq, uq[:, :, QK_NOPE:][:, :, perm]], axis=-1).reshape(Q_LORA, H * (dh + QK_ROPE))
        uq_ext = uq_ext.astype(BF16)
        ukv = w_ukv[l].astype(BF16)

        pa, gates = _in_proj(xs, sc1, sh1, w_all, n_plain=(wa_cols + wa_pad) // tn_in, tm=tm, tn=tn_in,
                             mod_fn=mod_tile)
        col_pool = Q_LORA // POOL_W
        col_gm = (Q_LORA + POOL_W) // GM_W
        col_kv = (Q_LORA + POOL_W + 2 * GM_W) // KV_LORA
        col_kr = (Q_LORA + POOL_W + 2 * GM_W + KV_LORA) // V7X_LANES

        hq = dh + QK_ROPE
        hb = min(HEADS_PER_STEP, H)
        qc = pl.pallas_call(
            functools.partial(_q_kernel, scale=scale),
            out_shape=jax.ShapeDtypeStruct((H, T, hq), BF16),
            grid=(T // tm, H // hb),
            in_specs=[
                pl.BlockSpec((tm, Q_LORA), lambda i, h: (i, 0)),
                pl.BlockSpec((None, 1, Q_LORA), lambda i, h: (0, 0, 0)),
                pl.BlockSpec((None, 1, Q_LORA), lambda i, h: (0, 0, 0)),
                pl.BlockSpec((Q_LORA, hb * hq), lambda i, h: (0, h)),
                pl.BlockSpec((tm, 2 * QK_ROPE), lambda i, h: (i, 0)),
                pl.BlockSpec((tm, 2 * QK_ROPE), lambda i, h: (i, 0)),
            ],
            out_specs=pl.BlockSpec((hb, tm, hq), lambda i, h: (h, i, 0)),
            scratch_shapes=[pltpu.VMEM((tm, Q_LORA), BF16)],
            compiler_params=_params(("parallel", "arbitrary"), 24),
            name="q_proj",
        )(pa, q_norm_g[l].reshape(1, 1, Q_LORA), zeros_q, uq_ext, cos_t, sin_t)
        kc, vc = pl.pallas_call(
            _kv_kernel,
            out_shape=(jax.ShapeDtypeStruct((H, T, hq), BF16), jax.ShapeDtypeStruct((H, T, 2 * V_DIM), BF16)),
            grid=(T // tm, H // hb),
            in_specs=[
                pl.BlockSpec((tm, KV_LORA), lambda i, h: (i, col_kv)),
                pl.BlockSpec((None, 1, KV_LORA), lambda i, h: (0, 0, 0)),
                pl.BlockSpec((None, 1, KV_LORA), lambda i, h: (0, 0, 0)),
                pl.BlockSpec((KV_LORA, hb * (QK_NOPE + V_DIM)), lambda i, h: (0, h)),
                pl.BlockSpec((tm, 2 * QK_ROPE), lambda i, h: (i, col_kr)),
                pl.BlockSpec((tm, 2 * QK_ROPE), lambda i, h: (i, 0)),
                pl.BlockSpec((tm, 2 * QK_ROPE), lambda i, h: (i, 0)),
            ],
            out_specs=(pl.BlockSpec((hb, tm, hq), lambda i, h: (h, i, 0)),
                       pl.BlockSpec((hb, tm, 2 * V_DIM), lambda i, h: (h, i, 0))),
            scratch_shapes=[pltpu.VMEM((tm, KV_LORA), BF16)],
            compiler_params=_params(("parallel", "arbitrary"), 24),
            name="kv_proj",
        )(pa, kv_norm_g[l].reshape(1, 1, KV_LORA), zeros_kv, ukv, pa, cos_t, sin_t)
        attn = _attention(qc, kc, vc, B=B, SEQ=SEQ, CTX=CTX)

        pool = _pool(pa, col_pool, w_pool[l].astype(BF16), pool_scale[l].reshape(1, POOL_W),
                     tm=tm, NL=NL, SEQ=SEQ, CTX=CTX)
        bias_full = jnp.repeat(jnp.transpose(b_spatial[l]), GM_W // GM_GROUPS, axis=1)
        gm = _gm(pa, col_gm, w_spatial[l].astype(BF16), bias_full, tm=tm)

        y = _branch(attn, pool, gm, _layer_bf16(w_branch, l), gates, tm=tm, tn=_pick(D, PROJ_COLS))
        xs = _mm_res(y, _layer_bf16(w_out, l), xs, g1, tm=tm, tn=_pick(D, PROJ_COLS), mod_fn=mod_tile)

        h2, logits_t = _mod_router(xs, sc2, sh2, jnp.transpose(w_router[l]).astype(BF16), tm=tm, mod_fn=mod_tile)
        wg, wu, wd = _layer_bf16(w_gate, l), _layer_bf16(w_up, l), _layer_bf16(w_down, l)
        xs = _moe(xs, logits_t, h2, g2, wg, wu, wd, n=SEQ, sets=B, row0=0, mod_row=None)
        if l != L - 1:
            xs = _moe(xs, logits_t, h2, g2, wg, wu, wd, n=CTX, sets=B, row0=NL, mod_row=B)

    return _final_norm(xs, final_g, NL, tm=tm).reshape(B, SEQ, D)
```
